```python
import math
import jax, jax.numpy as jnp
from jax import lax
import numpy as np

D_MODEL = 1024
BATCH = 32
SEQ = 2048
DEPTH = 1

EPS = 1e-6
D_FF = 2816
ATTN_HEADS = 8
ATTN_QK_DIM = 64
ATTN_V_DIM = 2 * ATTN_QK_DIM
ATTN_WIDTH = ATTN_HEADS * ATTN_V_DIM
QK_COLS = ATTN_HEADS * 2 * ATTN_QK_DIM
ROT_DIM = ATTN_QK_DIM // 4
ROPE_THETA = 500000.0
Q_BLOCK = 128
SSD_HEADS = 16
SSD_HEAD_DIM = 64
SSD_INNER = SSD_HEADS * SSD_HEAD_DIM
SSD_GROUPS = 2
SSD_STATE = 128
SSD_CONV = 5
SSD_CHUNK = 128
CONV_CH = SSD_INNER + 2 * SSD_GROUPS * SSD_STATE
D_MIX = ATTN_WIDTH + SSD_INNER
IN_SPLITS = (
    QK_COLS,
    2 * QK_COLS,
    2 * QK_COLS + ATTN_WIDTH,
    2 * QK_COLS + ATTN_WIDTH + SSD_INNER,
    2 * QK_COLS + ATTN_WIDTH + SSD_INNER + CONV_CH,
)
D_IN = 2 * QK_COLS + ATTN_WIDTH + SSD_INNER + CONV_CH + 2 * SSD_HEADS

kernel_name = "hybrid_diffattn_ssd_macaron_block"


def rmsnorm(x, g):
    xf = x.astype(jnp.float32)
    y = xf * lax.rsqrt(jnp.mean(xf * xf, axis=-1, keepdims=True) + EPS)
    return (y * g.astype(jnp.float32)).astype(x.dtype)


def swiglu(x, w_gate, w_up, w_down):
    return (jax.nn.silu(x @ w_gate) * (x @ w_up)) @ w_down


def rotary_tables(seq, dtype):
    pos = jnp.arange(seq, dtype=jnp.float32)
    inv_freq = jnp.power(jnp.float32(ROPE_THETA),
                         -jnp.arange(0, ROT_DIM, 2, dtype=jnp.float32) / ROT_DIM)
    ang = pos[:, None] * inv_freq[None, :]
    return jnp.cos(ang).astype(dtype), jnp.sin(ang).astype(dtype)


def apply_partial_rotary(t, cos, sin):
    half = ROT_DIM // 2
    c = cos[:, None, None, :]
    s = sin[:, None, None, :]
    x1 = t[..., :half]
    x2 = t[..., half:ROT_DIM]
    return jnp.concatenate([x1 * c - x2 * s, x2 * c + x1 * s, t[..., ROT_DIM:]], axis=-1)


def diff_attention(q, k, v, lam):
    b, s = q.shape[0], q.shape[1]
    nb = s // Q_BLOCK
    qb = q.reshape(b, nb, Q_BLOCK, ATTN_HEADS, 2, ATTN_QK_DIM).swapaxes(0, 1)
    scale = ATTN_QK_DIM ** -0.5

    def block(qi):
        sc = jnp.einsum("bqhcd,bkhcd->bhcqk", qi, k).astype(jnp.float32) * scale
        p = jax.nn.softmax(sc, axis=-1)
        p_diff = p[:, :, 0] - lam * p[:, :, 1]
        return jnp.einsum("bhqk,bkhe->bqhe", p_diff.astype(v.dtype), v)

    out = lax.map(block, qb)
    return out.swapaxes(0, 1).reshape(b, s, ATTN_HEADS, ATTN_V_DIM)


def centred_depthwise_conv(u, w, bias):
    out = lax.conv_general_dilated(
        u, w[:, None, :], window_strides=(1,),
        padding=[(SSD_CONV // 2, SSD_CONV // 2)],
        dimension_numbers=("NWC", "WIO", "NWC"),
        feature_group_count=u.shape[-1])
    return out + bias


def ssd_chunked(xh, dt, a, bm, cm):
    b, l, h, p = xh.shape
    g, n = bm.shape[2], bm.shape[3]
    r = h // g
    c = l // SSD_CHUNK
    L = SSD_CHUNK
    xdt = (xh * dt[..., None]).reshape(b, c, L, g, r, p)
    da = (dt * a).reshape(b, c, L, h).transpose(0, 1, 3, 2)
    cs = jnp.cumsum(da, axis=-1)
    bc = bm.reshape(b, c, L, g, n)
    cc = cm.reshape(b, c, L, g, n)
    mask = jnp.tril(jnp.ones((L, L), dtype=bool))
    diff = cs[..., :, None] - cs[..., None, :]
    decay = jnp.where(mask, jnp.exp(jnp.where(mask, diff, 0.0)), 0.0)
    decay = decay.reshape(b, c, g, r, L, L)
    cb = jnp.einsum("bclgn,bcsgn->bcgls", cc, bc)
    y_diag = jnp.einsum("bcgrls,bcsgrp->bclgrp", cb[:, :, :, None] * decay, xdt)
    decay_states = jnp.exp(cs[..., -1:] - cs).reshape(b, c, g, r, L)
    states = jnp.einsum("bclgn,bcgrl,bclgrp->bcgrpn", bc, decay_states, xdt)
    chunk_decay = jnp.exp(cs[..., -1]).reshape(b, c, g, r)

    def step(prev, inp):
        s_c, d_c = inp
        return prev * d_c[..., None, None] + s_c, prev

    init = jnp.zeros((b, g, r, p, n), dtype=states.dtype)
    _, prev_states = lax.scan(step, init,
                              (states.swapaxes(0, 1), chunk_decay.swapaxes(0, 1)))
    prev_states = prev_states.swapaxes(0, 1)
    state_decay_out = jnp.exp(cs).reshape(b, c, g, r, L)
    y_off = jnp.einsum("bclgn,bcgrpn,bcgrl->bclgrp", cc, prev_states, state_decay_out)
    return (y_diag + y_off).reshape(b, l, h, p)


def hybrid_mixer(hn, layer_idx, p):
    b, s, _ = hn.shape
    u = hn @ p["w_in"]
    q, k, v, z, xbc, dt_raw = jnp.split(u, IN_SPLITS, axis=-1)

    q = q.reshape(b, s, ATTN_HEADS, 2, ATTN_QK_DIM)
    k = k.reshape(b, s, ATTN_HEADS, 2, ATTN_QK_DIM)
    v = v.reshape(b, s, ATTN_HEADS, ATTN_V_DIM)
    cos, sin = rotary_tables(s, q.dtype)
    q = apply_partial_rotary(q, cos, sin)
    k = apply_partial_rotary(k, cos, sin)
    lam_init = 0.8 - 0.6 * math.exp(-0.3 * layer_idx)
    f32 = jnp.float32
    lam = (jnp.exp(jnp.sum(p["lambda_q1"].astype(f32) * p["lambda_k1"].astype(f32)))
           - jnp.exp(jnp.sum(p["lambda_q2"].astype(f32) * p["lambda_k2"].astype(f32)))
           + lam_init)
    attn = diff_attention(q, k, v, lam)
    attn = rmsnorm(attn, p["attn_subln_g"]) * (1.0 - lam_init)
    attn = attn.reshape(b, s, ATTN_WIDTH)

    xbc = jax.nn.silu(centred_depthwise_conv(xbc, p["conv_w"], p["conv_b"]))
    xs, bm, cm = jnp.split(xbc, (SSD_INNER, SSD_INNER + SSD_GROUPS * SSD_STATE), axis=-1)
    xs = xs.reshape(b, s, SSD_HEADS, SSD_HEAD_DIM)
    bm = bm.reshape(b, s, SSD_GROUPS, SSD_STATE)
    cm = cm.reshape(b, s, SSD_GROUPS, SSD_STATE)
    dt_f, dt_b = jnp.split(dt_raw.astype(f32), 2, axis=-1)
    dt_f = jax.nn.softplus(dt_f + p["dt_bias_fwd"].astype(f32))
    dt_b = jax.nn.softplus(dt_b + p["dt_bias_bwd"].astype(f32))
    a_f = -jnp.exp(p["a_log_fwd"].astype(f32))
    a_b = -jnp.exp(p["a_log_bwd"].astype(f32))
    y_f = ssd_chunked(xs, dt_f, a_f, bm, cm)
    rev = lambda t: jnp.flip(t, axis=1)
    y_b = rev(ssd_chunked(rev(xs), rev(dt_b), a_b, rev(bm), rev(cm)))
    y = y_f + y_b + xs * p["d_skip"][:, None]
    y = y.reshape(b, s, SSD_INNER).astype(hn.dtype)
    y = rmsnorm(y * jax.nn.silu(z), p["ssd_norm_g"])

    mixed = jnp.concatenate([attn, y], axis=-1)
    return mixed @ p["w_out"]


def hybrid_layer(h, layer_idx, p):
    f = swiglu(rmsnorm(h, p["ffn1_pre_g"]), p["ffn1_w_gate"], p["ffn1_w_up"], p["ffn1_w_down"])
    h = h + 0.5 * rmsnorm(f, p["ffn1_post_g"])
    m = hybrid_mixer(rmsnorm(h, p["mix_pre_g"]), layer_idx, p)
    h = h + rmsnorm(m, p["mix_post_g"])
    f = swiglu(rmsnorm(h, p["ffn2_pre_g"]), p["ffn2_w_gate"], p["ffn2_w_up"], p["ffn2_w_down"])
    h = h + 0.5 * rmsnorm(f, p["ffn2_post_g"])
    return rmsnorm(h, p["final_g"])


def setup_inputs(seed: int = 0) -> dict:
    key = jax.random.key(seed)
    ks = iter(jax.random.split(key, 40))
    f32 = jnp.float32

    def w(shape, fan_in):
        return jax.random.normal(next(ks), shape, f32) * (fan_in ** -0.5)

    def gain(shape):
        return 1.0 + 0.02 * jax.random.normal(next(ks), shape, f32)

    def small(shape, scale):
        return scale * jax.random.normal(next(ks), shape, f32)

    def a_log():
        return jnp.log(jax.random.uniform(next(ks), (DEPTH, SSD_HEADS), f32, 1.0, 16.0))

    def dt_bias():
        dt = jnp.exp(jax.random.uniform(next(ks), (DEPTH, SSD_HEADS), f32,
                                        math.log(1e-3), math.log(1e-1)))
        return dt + jnp.log(-jnp.expm1(-dt))

    Dp = DEPTH
    x = jax.random.normal(next(ks), (BATCH, SEQ, D_MODEL), f32)
    return {
        "x": x,
        "ffn1_pre_g": gain((Dp, D_MODEL)),
        "ffn1_w_gate": w((Dp, D_MODEL, D_FF), D_MODEL),
        "ffn1_w_up": w((Dp, D_MODEL, D_FF), D_MODEL),
        "ffn1_w_down": w((Dp, D_FF, D_MODEL), D_FF),
        "ffn1_post_g": gain((Dp, D_MODEL)),
        "mix_pre_g": gain((Dp, D_MODEL)),
        "w_in": w((Dp, D_MODEL, D_IN), D_MODEL),
        "lambda_q1": small((Dp, ATTN_QK_DIM), 0.1),
        "lambda_k1": small((Dp, ATTN_QK_DIM), 0.1),
        "lambda_q2": small((Dp, ATTN_QK_DIM), 0.1),
        "lambda_k2": small((Dp, ATTN_QK_DIM), 0.1),
        "attn_subln_g": gain((Dp, ATTN_V_DIM)),
        "conv_w": w((Dp, SSD_CONV, CONV_CH), SSD_CONV),
        "conv_b": small((Dp, CONV_CH), 0.02),
        "a_log_fwd": a_log(),
        "a_log_bwd": a_log(),
        "dt_bias_fwd": dt_bias(),
        "dt_bias_bwd": dt_bias(),
        "d_skip": gain((Dp, SSD_HEADS)),
        "ssd_norm_g": gain((Dp, SSD_INNER)),
        "w_out": w((Dp, D_MIX, D_MODEL), D_MIX),
        "mix_post_g": gain((Dp, D_MODEL)),
        "ffn2_pre_g": gain((Dp, D_MODEL)),
        "ffn2_w_gate": w((Dp, D_MODEL, D_FF), D_MODEL),
        "ffn2_w_up": w((Dp, D_MODEL, D_FF), D_MODEL),
        "ffn2_w_down": w((Dp, D_FF, D_MODEL), D_FF),
        "ffn2_post_g": gain((Dp, D_MODEL)),
        "final_g": gain((Dp, D_MODEL)),
    }


def reference(x, ffn1_pre_g, ffn1_w_gate, ffn1_w_up, ffn1_w_down, ffn1_post_g,
              mix_pre_g, w_in, lambda_q1, lambda_k1, lambda_q2, lambda_k2,
              attn_subln_g, conv_w, conv_b, a_log_fwd, a_log_bwd, dt_bias_fwd,
              dt_bias_bwd, d_skip, ssd_norm_g, w_out, mix_post_g, ffn2_pre_g,
              ffn2_w_gate, ffn2_w_up, ffn2_w_down, ffn2_post_g, final_g):
    h = x
    for i in range(DEPTH):
        p = {
            "ffn1_pre_g": ffn1_pre_g[i], "ffn1_w_gate": ffn1_w_gate[i],
            "ffn1_w_up": ffn1_w_up[i], "ffn1_w_down": ffn1_w_down[i],
            "ffn1_post_g": ffn1_post_g[i], "mix_pre_g": mix_pre_g[i],
            "w_in": w_in[i], "lambda_q1": lambda_q1[i], "lambda_k1": lambda_k1[i],
            "lambda_q2": lambda_q2[i], "lambda_k2": lambda_k2[i],
            "attn_subln_g": attn_subln_g[i], "conv_w": conv_w[i], "conv_b": conv_b[i],
            "a_log_fwd": a_log_fwd[i], "a_log_bwd": a_log_bwd[i],
            "dt_bias_fwd": dt_bias_fwd[i], "dt_bias_bwd": dt_bias_bwd[i],
            "d_skip": d_skip[i], "ssd_norm_g": ssd_norm_g[i], "w_out": w_out[i],
            "mix_post_g": mix_post_g[i], "ffn2_pre_g": ffn2_pre_g[i],
            "ffn2_w_gate": ffn2_w_gate[i], "ffn2_w_up": ffn2_w_up[i],
            "ffn2_w_down": ffn2_w_down[i], "ffn2_post_g": ffn2_post_g[i],
            "final_g": final_g[i],
        }
        h = hybrid_layer(h, i, p)
    return h
```

```python
import functools
import math

import jax
import jax.numpy as jnp
from jax import lax
from jax.experimental import pallas as pl
from jax.experimental.pallas import tpu as pltpu

F32 = jnp.float32
BF16 = jnp.bfloat16

EPS = 1e-6
ATTN_HEADS = 8
ATTN_QK_DIM = 64
ATTN_V_DIM = 2 * ATTN_QK_DIM
ROT_DIM = ATTN_QK_DIM // 4
ROPE_THETA = 500000.0
SSD_HEADS = 16
SSD_HEAD_DIM = 64
SSD_GROUPS = 2
SSD_STATE = 128
SSD_CONV = 5
SSD_CHUNK = 128

LANES = 128
HALO_ROWS = 16
VMEM_LIMIT_BYTES = 56 * 1024 * 1024


def _params(*semantics):
    return pltpu.CompilerParams(dimension_semantics=semantics,
                                vmem_limit_bytes=VMEM_LIMIT_BYTES)


def _resident(shape):
    zeros = (0,) * len(shape)
    return pl.BlockSpec(shape, lambda *_: zeros, pipeline_mode=pl.Buffered(1))


def _rms(x, g):
    return x * lax.rsqrt(jnp.mean(x * x, axis=-1, keepdims=True) + EPS) * g


def _silu(x):
    return x * jax.nn.sigmoid(x)


def _dot(a, b):
    return jnp.dot(a, b, preferred_element_type=F32)


def _dot_nt(a, b):
    return lax.dot_general(a, b, (((1,), (1,)), ((), ())), preferred_element_type=F32)


def _swiglu_half_step(x, gpre, wg_ref, wu_ref, wd_ref, gpost):
    xn = _rms(x, gpre).astype(BF16)
    g = _dot(xn, wg_ref[...])
    u = _dot(xn, wu_ref[...])
    a = (_silu(g) * u).astype(BF16)
    f = _dot(a, wd_ref[...])
    return x + 0.5 * _rms(f, gpost)


def _ffn_kernel(x_ref, gpre_ref, wg_ref, wu_ref, wd_ref, gpost_ref, o_ref):
    o_ref[...] = _swiglu_half_step(x_ref[...], gpre_ref[...], wg_ref, wu_ref, wd_ref,
                                   gpost_ref[...])


def _ffn(x2d, gpre, wg, wu, wd, gpost, tm):
    t, d = x2d.shape
    f = wg.shape[1]
    row = pl.BlockSpec((tm, d), lambda i: (i, 0))
    return pl.pallas_call(
        _ffn_kernel,
        grid=(t // tm,),
        in_specs=[row, _resident((1, d)), _resident((d, f)), _resident((d, f)),
                  _resident((f, d)), _resident((1, d))],
        out_specs=row,
        out_shape=jax.ShapeDtypeStruct((t, d), F32),
        compiler_params=_params("parallel"),
        name="ffn1",
    )(x2d, gpre, wg, wu, wd, gpost)


def _in_proj_kernel(h_ref, g_ref, w_ref, cos_ref, sa_ref, sb_ref, dtb_ref,
                    q_ref, k_ref, v_ref, z_ref, xbc_ref, dt_ref, *, qk_cols, v_cols,
                    z_cols, xbc_cols):
    hn = _rms(h_ref[...], g_ref[...]).astype(BF16)
    cos = cos_ref[...]
    sa = sa_ref[...]
    sb = sb_ref[...]
    scale = ATTN_QK_DIM ** -0.5

    def rotary_cols(col0, out_ref, mult):
        for c in range(qk_cols // LANES):
            t = _dot(hn, w_ref[:, col0 + c * LANES:col0 + (c + 1) * LANES])
            r = t * cos + pltpu.roll(t, LANES - ROT_DIM // 2, 1) * sa \
                + pltpu.roll(t, ROT_DIM // 2, 1) * sb
            if mult != 1.0:
                r = r * mult
            out_ref[:, c * LANES:(c + 1) * LANES] = r.astype(out_ref.dtype)

    rotary_cols(0, q_ref, scale)
    rotary_cols(qk_cols, k_ref, 1.0)
    c0 = 2 * qk_cols
    v_ref[...] = _dot(hn, w_ref[:, c0:c0 + v_cols]).astype(v_ref.dtype)
    c0 += v_cols
    z_ref[...] = _dot(hn, w_ref[:, c0:c0 + z_cols]).astype(z_ref.dtype)
    c0 += z_cols
    xbc_ref[...] = _dot(hn, w_ref[:, c0:c0 + xbc_cols]).astype(xbc_ref.dtype)
    c0 += xbc_cols
    dt_raw = _dot(hn, w_ref[:, c0:c0 + LANES]) + dtb_ref[...]
    dt_ref[...] = jax.nn.softplus(dt_raw)


def _in_proj(h2d, g, w_pad, cos, sa, sb, dtb, tm, seq, qk_cols, v_cols, z_cols, xbc_cols):
    t, d = h2d.shape
    n_pos_blocks = seq // tm
    row = lambda n: pl.BlockSpec((tm, n), lambda i: (i, 0))
    pos = pl.BlockSpec((tm, LANES), lambda i: (i % n_pos_blocks, 0))
    kern = functools.partial(_in_proj_kernel, qk_cols=qk_cols, v_cols=v_cols,
                             z_cols=z_cols, xbc_cols=xbc_cols)
    return pl.pallas_call(
        kern,
        grid=(t // tm,),
        in_specs=[row(d), _resident((1, d)), _resident(w_pad.shape), pos, pos, pos,
                  _resident((1, LANES))],
        out_specs=[row(qk_cols), row(qk_cols), row(v_cols), row(z_cols), row(xbc_cols),
                   row(LANES)],
        out_shape=[jax.ShapeDtypeStruct((t, qk_cols), BF16),
                   jax.ShapeDtypeStruct((t, qk_cols), BF16),
                   jax.ShapeDtypeStruct((t, v_cols), BF16),
                   jax.ShapeDtypeStruct((t, z_cols), BF16),
                   jax.ShapeDtypeStruct((t, xbc_cols), BF16),
                   jax.ShapeDtypeStruct((t, LANES), F32)],
        compiler_params=_params("parallel"),
        name="in_proj",
    )(h2d, g, w_pad, cos, sa, sb, dtb)


def _conv_kernel(prev_ref, cur_ref, next_ref, w_ref, b_ref, o_ref, *, tc, col_block):
    i = pl.program_id(1)
    last = pl.num_programs(1) - 1
    half = HALO_ROWS // 2
    pad = SSD_CONV // 2
    n_ch = cur_ref.shape[-1]
    for c0 in range(0, n_ch, col_block):
        cols = slice(c0, c0 + col_block)
        prev = prev_ref[0, half:, cols].astype(F32)
        prev = jnp.where(i > 0, prev, 0.0)
        nxt = next_ref[0, :half, cols].astype(F32)
        nxt = jnp.where(i < last, nxt, 0.0)
        ext = jnp.concatenate([prev, cur_ref[0, :, cols].astype(F32), nxt], axis=0)
        acc = jnp.zeros((tc, col_block), F32)
        for k in range(SSD_CONV):
            off = half - pad + k
            acc = acc + ext[off:off + tc] * w_ref[k:k + 1, cols]
        acc = acc + b_ref[:, cols]
        o_ref[0, :, cols] = _silu(acc).astype(o_ref.dtype)


def _conv(xbc, w, b, tc):
    bsz, seq, n_ch = xbc.shape
    per = tc // HALO_ROWS
    n_halo = seq // HALO_ROWS
    cur = pl.BlockSpec((1, tc, n_ch), lambda bi, i: (bi, i, 0))
    prev = pl.BlockSpec((1, HALO_ROWS, n_ch),
                        lambda bi, i: (bi, jnp.maximum(i * per - 1, 0), 0))
    nxt = pl.BlockSpec((1, HALO_ROWS, n_ch),
                       lambda bi, i: (bi, jnp.minimum((i + 1) * per, n_halo - 1), 0))
    kern = functools.partial(_conv_kernel, tc=tc, col_block=2 * LANES)
    return pl.pallas_call(
        kern,
        grid=(bsz, seq // tc),
        in_specs=[prev, cur, nxt, _resident(w.shape), _resident(b.shape)],
        out_specs=cur,
        out_shape=jax.ShapeDtypeStruct(xbc.shape, BF16),
        compiler_params=_params("parallel", "parallel"),
        name="conv",
    )(xbc, xbc, xbc, w, b)


def _attn_kernel(q_ref, k_ref, v_ref, lam_ref, g_ref, o_ref, *, lam_init):
    q = q_ref[0]
    k = k_ref[0]
    v = v_ref[0]
    lp = lam_ref[...]
    lam = (jnp.exp(jnp.sum(lp[0:1] * lp[1:2], axis=-1, keepdims=True))
           - jnp.exp(jnp.sum(lp[2:3] * lp[3:4], axis=-1, keepdims=True)) + lam_init)
    lane = lax.broadcasted_iota(jnp.int32, q.shape, 1)
    first = lane < ATTN_QK_DIM
    zero = jnp.zeros_like(q)

    def softmax_map(qm):
        s = _dot_nt(qm, k)
        p = jnp.exp(s - jnp.max(s, axis=-1, keepdims=True))
        return p, jnp.sum(p, axis=-1, keepdims=True)

    p1, l1 = softmax_map(jnp.where(first, q, zero))
    p2, l2 = softmax_map(jnp.where(first, zero, q))
    pd = p1 * (1.0 / l1) - p2 * (lam / l2)
    o = _dot(pd.astype(BF16), v)
    o = _rms(o, g_ref[...]) * (1.0 - lam_init)
    o_ref[0] = o.astype(o_ref.dtype)


def _attention(q, k, v, lam_params, g, tq, lam_init):
    bsz, seq, width = q.shape
    heads = width // ATTN_V_DIM
    qo = pl.BlockSpec((1, tq, ATTN_V_DIM), lambda b, h, i: (b, i, h))
    kv = pl.BlockSpec((1, seq, ATTN_V_DIM), lambda b, h, i: (b, 0, h))
    kern = functools.partial(_attn_kernel, lam_init=lam_init)
    return pl.pallas_call(
        kern,
        grid=(bsz, heads, seq // tq),
        in_specs=[qo, kv, kv, _resident(lam_params.shape), _resident(g.shape)],
        out_specs=qo,
        out_shape=jax.ShapeDtypeStruct(q.shape, BF16),
        compiler_params=_params("parallel", "parallel", "parallel"),
        name="attention",
    )(q, k, v, lam_params, g)


def _cumsum_rows(x, reverse):
    n = x.shape[0]
    row = lax.broadcasted_iota(jnp.int32, x.shape, 0)
    step = 1
    while step < n:
        if reverse:
            shifted = pltpu.roll(x, n - step, 0)
            x = x + jnp.where(row < n - step, shifted, 0.0)
        else:
            shifted = pltpu.roll(x, step, 0)
            x = x + jnp.where(row >= step, shifted, 0.0)
        step *= 2
    return x


def _ssd_direction(xc_ref, dt_ref, alog_ref, state_ref, y_ref, *, lane0, reverse):
    L = SSD_CHUNK
    n_x = SSD_HEADS * SSD_HEAD_DIM
    gw = SSD_STATE
    dt = dt_ref[0]
    a = -jnp.exp(alog_ref[...])
    cs = _cumsum_rows(dt * a, reverse)
    tot = cs[0:1] if reverse else cs[L - 1:L]
    w = dt * jnp.exp(tot - cs)
    ecs = jnp.exp(cs)
    cs_t = cs.T
    dt_t = dt.T
    w_t = w.T
    row = lax.broadcasted_iota(jnp.int32, (L, L), 0)
    col = lax.broadcasted_iota(jnp.int32, (L, L), 1)
    keep = (row <= col) if reverse else (row >= col)
    lane = lax.broadcasted_iota(jnp.int32, (L, LANES), 1)
    first = lane < SSD_HEAD_DIM

    cb, b_t, c_f = [], [], []
    for g in range(SSD_GROUPS):
        b_g = xc_ref[0, :, n_x + g * gw:n_x + (g + 1) * gw]
        c_g = xc_ref[0, :, n_x + SSD_GROUPS * gw + g * gw:n_x + SSD_GROUPS * gw + (g + 1) * gw]
        cb.append(_dot_nt(c_g, b_g))
        b_t.append(b_g.astype(F32).T)
        c_f.append(c_g.astype(F32))

    heads_per_group = SSD_HEADS // SSD_GROUPS
    for pair in range(SSD_HEADS // 2):
        cols = slice(pair * LANES, (pair + 1) * LANES)
        xs = xc_ref[0, :, cols]
        prev = state_ref[:, cols]
        rhs = jnp.concatenate([xs, prev.astype(BF16)], axis=0)
        ys, ss, cds = [], [], []
        for h in (2 * pair, 2 * pair + 1):
            g = h // heads_per_group
            hl = lane0 + h
            col_cs = jnp.broadcast_to(cs[:, hl:hl + 1], (L, L))
            row_cs = jnp.broadcast_to(cs_t[hl:hl + 1, :], (L, L))
            row_dt = jnp.broadcast_to(dt_t[hl:hl + 1, :], (L, L))
            decay = jnp.where(keep, jnp.exp(col_cs - row_cs), 0.0)
            m = (cb[g] * decay * row_dt).astype(BF16)
            c_e = (c_f[g] * jnp.broadcast_to(ecs[:, hl:hl + 1], (L, gw))).astype(BF16)
            ys.append(_dot(jnp.concatenate([m, c_e], axis=1), rhs))
            b_w = (b_t[g] * jnp.broadcast_to(w_t[hl:hl + 1, :], (gw, L))).astype(BF16)
            ss.append(_dot(b_w, xs))
            cds.append(jnp.broadcast_to(jnp.exp(tot[:, hl:hl + 1]), (1, LANES)))
        first_row = first[0:1]
        y_ref[0, :, cols] = jnp.where(first, ys[0], ys[1]).astype(y_ref.dtype)
        chunk_decay = jnp.where(first_row, cds[0], cds[1])
        state_ref[:, cols] = prev * chunk_decay + jnp.where(first, ss[0], ss[1])


def _ssd_kernel(xf_ref, xb_ref, dtf_ref, dtb_ref, alog_ref, yf_ref, yb_ref, sf_ref, sb_ref):
    @pl.when(pl.program_id(1) == 0)
    def _():
        sf_ref[...] = jnp.zeros_like(sf_ref)
        sb_ref[...] = jnp.zeros_like(sb_ref)

    _ssd_direction(xf_ref, dtf_ref, alog_ref, sf_ref, yf_ref, lane0=0, reverse=False)
    _ssd_direction(xb_ref, dtb_ref, alog_ref, sb_ref, yb_ref, lane0=SSD_HEADS, reverse=True)


def _ssd(xconv, dt, alog):
    bsz, seq, n_ch = xconv.shape
    n_x = SSD_HEADS * SSD_HEAD_DIM
    nc = seq // SSD_CHUNK
    fwd = lambda n: pl.BlockSpec((1, SSD_CHUNK, n), lambda b, t: (b, t, 0))
    bwd = lambda n: pl.BlockSpec((1, SSD_CHUNK, n), lambda b, t: (b, nc - 1 - t, 0))
    y_shape = jax.ShapeDtypeStruct((bsz, seq, n_x), BF16)
    return pl.pallas_call(
        _ssd_kernel,
        grid=(bsz, nc),
        in_specs=[fwd(n_ch), bwd(n_ch), fwd(LANES), bwd(LANES), _resident(alog.shape)],
        out_specs=[fwd(n_x), bwd(n_x)],
        out_shape=[y_shape, y_shape],
        scratch_shapes=[pltpu.VMEM((SSD_STATE, n_x), F32), pltpu.VMEM((SSD_STATE, n_x), F32)],
        compiler_params=_params("parallel", "arbitrary"),
        name="ssd",
    )(xconv, xconv, dt, dt, alog)


def _out_ffn_kernel(h_ref, attn_ref, yf_ref, yb_ref, xs_ref, z_ref, dskip_ref, gssd_ref,
                    wo_ref, gmix_ref, gpre_ref, wg_ref, wu_ref, wd_ref, gpost_ref,
                    gfinal_ref, o_ref):
    n_attn = attn_ref.shape[-1]
    y = (yf_ref[...].astype(F32) + yb_ref[...].astype(F32)
         + xs_ref[...].astype(F32) * dskip_ref[...])
    y = _rms(y * _silu(z_ref[...].astype(F32)), gssd_ref[...])
    m = _dot(attn_ref[...], wo_ref[:n_attn, :]) + _dot(y.astype(BF16), wo_ref[n_attn:, :])
    h = h_ref[...] + _rms(m, gmix_ref[...])
    h = _swiglu_half_step(h, gpre_ref[...], wg_ref, wu_ref, wd_ref, gpost_ref[...])
    o_ref[...] = _rms(h, gfinal_ref[...])


def _out_ffn(h2d, attn, yf, yb, xconv, z, dskip, gssd, wo, gmix, gpre, wg, wu, wd, gpost,
             gfinal, tm):
    t, d = h2d.shape
    f = wg.shape[1]
    n_x = yf.shape[-1]
    row = lambda n: pl.BlockSpec((tm, n), lambda i: (i, 0))
    return pl.pallas_call(
        _out_ffn_kernel,
        grid=(t // tm,),
        in_specs=[row(d), row(attn.shape[-1]), row(n_x), row(n_x), row(n_x), row(n_x),
                  _resident((1, n_x)), _resident((1, n_x)), _resident(wo.shape),
                  _resident((1, d)), _resident((1, d)), _resident((d, f)),
                  _resident((d, f)), _resident((f, d)), _resident((1, d)),
                  _resident((1, d))],
        out_specs=row(d),
        out_shape=jax.ShapeDtypeStruct((t, d), F32),
        compiler_params=_params("parallel"),
        name="out_ffn",
    )(h2d, attn, yf, yb, xconv, z, dskip, gssd, wo, gmix, gpre, wg, wu, wd, gpost, gfinal)


def _rotary_tables(seq):
    half = ROT_DIM // 2
    pos = jnp.arange(seq, dtype=F32)
    inv_freq = jnp.power(F32(ROPE_THETA), -jnp.arange(0, ROT_DIM, 2, dtype=F32) / ROT_DIM)
    ang = pos[:, None] * inv_freq[None, :]
    cos, sin = jnp.cos(ang), jnp.sin(ang)
    ones = jnp.ones((seq, ATTN_QK_DIM - ROT_DIM), F32)
    zeros_h = jnp.zeros((seq, half), F32)
    zeros_r = jnp.zeros((seq, ATTN_QK_DIM - ROT_DIM), F32)
    reps = LANES // ATTN_QK_DIM
    cos_t = jnp.tile(jnp.concatenate([cos, cos, ones], axis=1), (1, reps))
    sa_t = jnp.tile(jnp.concatenate([-sin, zeros_h, zeros_r], axis=1), (1, reps))
    sb_t = jnp.tile(jnp.concatenate([zeros_h, sin, zeros_r], axis=1), (1, reps))
    return cos_t, sa_t, sb_t


def _pad_lanes(v, width=LANES):
    return jnp.pad(v, ((0, 0), (0, width - v.shape[-1])))


def _layer(h2d, layer_idx, p, bsz, seq):
    t, d = h2d.shape
    tm = min(512, t)
    row = lambda v: v.reshape(1, -1).astype(F32)
    bf = lambda w: w.astype(BF16)

    h1 = _ffn(h2d, row(p["ffn1_pre_g"]), bf(p["ffn1_w_gate"]), bf(p["ffn1_w_up"]),
              bf(p["ffn1_w_down"]), row(p["ffn1_post_g"]), tm)

    qk_cols = ATTN_HEADS * 2 * ATTN_QK_DIM
    v_cols = ATTN_HEADS * ATTN_V_DIM
    n_x = SSD_HEADS * SSD_HEAD_DIM
    xbc_cols = n_x + 2 * SSD_GROUPS * SSD_STATE
    w_in = p["w_in"]
    w_pad = bf(jnp.pad(w_in, ((0, 0), (0, 2 * qk_cols + v_cols + n_x + xbc_cols + LANES
                                          - w_in.shape[1]))))
    dt_bias = _pad_lanes(jnp.concatenate([p["dt_bias_fwd"], p["dt_bias_bwd"]]).reshape(1, -1)
                         .astype(F32))
    alog = _pad_lanes(jnp.concatenate([p["a_log_fwd"], p["a_log_bwd"]]).reshape(1, -1)
                      .astype(F32))
    cos_t, sa_t, sb_t = _rotary_tables(seq)
    tp = min(tm, seq)
    q, k, v, z, xbc, dt = _in_proj(h1, row(p["mix_pre_g"]), w_pad, cos_t, sa_t, sb_t, dt_bias,
                                   tp, seq, qk_cols, v_cols, n_x, xbc_cols)

    shape3 = lambda a: a.reshape(bsz, seq, a.shape[-1])
    xconv = _conv(shape3(xbc), p["conv_w"].astype(F32), row(p["conv_b"]), min(512, seq))

    lam_init = 0.8 - 0.6 * math.exp(-0.3 * layer_idx)
    lam_params = jnp.stack([p["lambda_q1"], p["lambda_k1"], p["lambda_q2"],
                            p["lambda_k2"]]).astype(F32)
    attn = _attention(shape3(q), shape3(k), shape3(v), lam_params, row(p["attn_subln_g"]),
                      min(512, seq), lam_init)

    yf, yb = _ssd(xconv, shape3(dt), alog)

    dskip = jnp.repeat(p["d_skip"].astype(F32), SSD_HEAD_DIM).reshape(1, -1)
    flat = lambda a: a.reshape(t, a.shape[-1])
    return _out_ffn(h1, flat(attn), flat(yf), flat(yb), flat(xconv), z, dskip,
                    row(p["ssd_norm_g"]), bf(p["w_out"]), row(p["mix_post_g"]),
                    row(p["ffn2_pre_g"]), bf(p["ffn2_w_gate"]), bf(p["ffn2_w_up"]),
                    bf(p["ffn2_w_down"]), row(p["ffn2_post_g"]), row(p["final_g"]), tm)


def kernel(x, ffn1_pre_g, ffn1_w_gate, ffn1_w_up, ffn1_w_down, ffn1_post_g, mix_pre_g, w_in, lambda_q1, lambda_k1, lambda_q2, lambda_k2, attn_subln_g, conv_w, conv_b, a_log_fwd, a_log_bwd, dt_bias_fwd, dt_bias_bwd, d_skip, ssd_norm_g, w_out, mix_post_g, ffn2_pre_g, ffn2_w_gate, ffn2_w_up, ffn2_w_down, ffn2_post_g, final_g):
    names = ("ffn1_pre_g", "ffn1_w_gate", "ffn1_w_up", "ffn1_w_down", "ffn1_post_g",
             "mix_pre_g", "w_in", "lambda_q1", "lambda_k1", "lambda_q2", "lambda_k2",
             "attn_subln_g", "conv_w", "conv_b", "a_log_fwd", "a_log_bwd", "dt_bias_fwd",
             "dt_bias_bwd", "d_skip", "ssd_norm_g", "w_out", "mix_post_g", "ffn2_pre_g",
             "ffn2_w_gate", "ffn2_w_up", "ffn2_w_down", "ffn2_post_g", "final_g")
    stacked = dict(zip(names, (ffn1_pre_g, ffn1_w_gate, ffn1_w_up, ffn1_w_down, ffn1_post_g,
                               mix_pre_g, w_in, lambda_q1, lambda_k1, lambda_q2, lambda_k2,
                               attn_subln_g, conv_w, conv_b, a_log_fwd, a_log_bwd,
                               dt_bias_fwd, dt_bias_bwd, d_skip, ssd_norm_g, w_out,
                               mix_post_g, ffn2_pre_g, ffn2_w_gate, ffn2_w_up, ffn2_w_down,
                               ffn2_post_g, final_g)))
    bsz, seq, d = x.shape
    h = x.reshape(bsz * seq, d)
    for i in range(ffn1_pre_g.shape[0]):
        h = _layer(h, i, {n: a[i] for n, a in stacked.items()}, bsz, seq)
    return h.reshape(bsz, seq, d)
```

```python
import functools
import math

import jax
import jax.numpy as jnp
from jax import lax
from jax.experimental import pallas as pl
from jax.experimental.pallas import tpu as pltpu

F32 = jnp.float32
BF16 = jnp.bfloat16

EPS = 1e-6
ATTN_HEADS = 8
ATTN_QK_DIM = 64
ATTN_V_DIM = 2 * ATTN_QK_DIM
ROT_DIM = ATTN_QK_DIM // 4
ROPE_THETA = 500000.0
SSD_HEADS = 16
SSD_HEAD_DIM = 64
SSD_GROUPS = 2
SSD_STATE = 128
SSD_CONV = 5
SSD_CHUNK = 128

LANES = 128
MXU_COLS = 256
BF16_ROWS = 16
LOG2_E = math.log2(math.e)
ATTN_QB = MXU_COLS
ATTN_KB = MXU_COLS
HALO_ROWS = 16
VMEM_LIMIT_BYTES = 56 * 1024 * 1024


def _params(*semantics):
    return pltpu.CompilerParams(dimension_semantics=semantics,
                                vmem_limit_bytes=VMEM_LIMIT_BYTES)


def _resident(shape):
    zeros = (0,) * len(shape)
    return pl.BlockSpec(shape, lambda *_: zeros, pipeline_mode=pl.Buffered(1))


def _rms(x, g):
    return x * lax.rsqrt(jnp.mean(x * x, axis=-1, keepdims=True) + EPS) * g


def _silu(x):
    return x * jax.nn.sigmoid(x)


def _dot(a, b):
    return jnp.dot(a, b, preferred_element_type=F32)


def _dot_nt(a, b):
    return lax.dot_general(a, b, (((1,), (1,)), ((), ())), preferred_element_type=F32)


def _swiglu_half_step(x, gpre, wg_ref, wu_ref, wd_ref, gpost):
    xn = _rms(x, gpre).astype(BF16)
    g = _dot(xn, wg_ref[...])
    u = _dot(xn, wu_ref[...])
    a = (_silu(g) * u).astype(BF16)
    f = _dot(a, wd_ref[...])
    return x + 0.5 * _rms(f, gpost)


def _ffn_kernel(x_ref, gpre_ref, wg_ref, wu_ref, wd_ref, gpost_ref, o_ref):
    o_ref[...] = _swiglu_half_step(x_ref[...], gpre_ref[...], wg_ref, wu_ref, wd_ref,
                                   gpost_ref[...])


def _ffn(x2d, gpre, wg, wu, wd, gpost, tm):
    t, d = x2d.shape
    f = wg.shape[1]
    row = pl.BlockSpec((tm, d), lambda i: (i, 0))
    return pl.pallas_call(
        _ffn_kernel,
        grid=(t // tm,),
        in_specs=[row, _resident((1, d)), _resident((d, f)), _resident((d, f)),
                  _resident((f, d)), _resident((1, d))],
        out_specs=row,
        out_shape=jax.ShapeDtypeStruct((t, d), F32),
        compiler_params=_params("parallel"),
        name="ffn1",
    )(x2d, gpre, wg, wu, wd, gpost)


def _in_proj_kernel(h_ref, g_ref, w_ref, wvt_ref, cos_ref, sa_ref, sb_ref, dtb_ref,
                    q_ref, k_ref, vt_ref, z_ref, xbc_ref, dt_ref, *, qk_cols, z_cols,
                    xbc_cols):
    hn = _rms(h_ref[...], g_ref[...]).astype(BF16)
    cos = cos_ref[...]
    sa = sa_ref[...]
    sb = sb_ref[...]

    def rotary_cols(col0, out_ref, mult):
        for c in range(qk_cols // MXU_COLS):
            t2 = _dot(hn, w_ref[:, col0 + c * MXU_COLS:col0 + (c + 1) * MXU_COLS])
            for half in range(MXU_COLS // LANES):
                t = t2[:, half * LANES:(half + 1) * LANES]
                r = t * cos + pltpu.roll(t, LANES - ROT_DIM // 2, 1) * sa \
                    + pltpu.roll(t, ROT_DIM // 2, 1) * sb
                lo = c * MXU_COLS + half * LANES
                out_ref[:, lo:lo + LANES] = (r * mult).astype(out_ref.dtype)

    rotary_cols(0, q_ref, ATTN_QK_DIM ** -0.5 * LOG2_E)
    rotary_cols(qk_cols, k_ref, 1.0)
    vt_ref[0] = _dot_nt(wvt_ref[...], hn).astype(vt_ref.dtype)
    c0 = 2 * qk_cols
    z_ref[...] = _dot(hn, w_ref[:, c0:c0 + z_cols]).astype(z_ref.dtype)
    c0 += z_cols
    xbc_ref[...] = _dot(hn, w_ref[:, c0:c0 + xbc_cols]).astype(xbc_ref.dtype)
    c0 += xbc_cols
    dt_raw = _dot(hn, w_ref[:, c0:c0 + LANES]) + dtb_ref[...]
    dt_ref[...] = jax.nn.softplus(dt_raw)


def _in_proj(h2d, g, w_pad, wvt, cos, sa, sb, dtb, tm, seq, qk_cols, z_cols, xbc_cols):
    t, d = h2d.shape
    v_cols = wvt.shape[0]
    nb = seq // tm
    row = lambda n: pl.BlockSpec((tm, n), lambda i: (i, 0))
    pos = pl.BlockSpec((tm, LANES), lambda i: (i % nb, 0))
    vt_spec = pl.BlockSpec((1, v_cols, tm), lambda i: (i // nb, 0, i % nb))
    kern = functools.partial(_in_proj_kernel, qk_cols=qk_cols, z_cols=z_cols,
                             xbc_cols=xbc_cols)
    return pl.pallas_call(
        kern,
        grid=(t // tm,),
        in_specs=[row(d), _resident((1, d)), _resident(w_pad.shape), _resident(wvt.shape),
                  pos, pos, pos, _resident((1, LANES))],
        out_specs=[row(qk_cols), row(qk_cols), vt_spec, row(z_cols), row(xbc_cols),
                   row(LANES)],
        out_shape=[jax.ShapeDtypeStruct((t, qk_cols), BF16),
                   jax.ShapeDtypeStruct((t, qk_cols), BF16),
                   jax.ShapeDtypeStruct((t // seq, v_cols, seq), BF16),
                   jax.ShapeDtypeStruct((t, z_cols), BF16),
                   jax.ShapeDtypeStruct((t, xbc_cols), BF16),
                   jax.ShapeDtypeStruct((t, LANES), F32)],
        compiler_params=_params("parallel"),
        name="in_proj",
    )(h2d, g, w_pad, wvt, cos, sa, sb, dtb)


def _conv_kernel(prev_ref, cur_ref, next_ref, w_ref, b_ref, o_ref, *, tc, col_block):
    i = pl.program_id(1)
    last = pl.num_programs(1) - 1
    half = HALO_ROWS // 2
    pad = SSD_CONV // 2
    n_ch = cur_ref.shape[-1]
    for c0 in range(0, n_ch, col_block):
        cols = slice(c0, c0 + col_block)
        prev = prev_ref[0, half:, cols].astype(F32)
        prev = jnp.where(i > 0, prev, 0.0)
        nxt = next_ref[0, :half, cols].astype(F32)
        nxt = jnp.where(i < last, nxt, 0.0)
        ext = jnp.concatenate([prev, cur_ref[0, :, cols].astype(F32), nxt], axis=0)
        acc = jnp.zeros((tc, col_block), F32)
        for k in range(SSD_CONV):
            off = half - pad + k
            acc = acc + ext[off:off + tc] * w_ref[k:k + 1, cols]
        acc = acc + b_ref[:, cols]
        o_ref[0, :, cols] = _silu(acc).astype(o_ref.dtype)


def _conv(xbc, w, b, tc):
    bsz, seq, n_ch = xbc.shape
    per = tc // HALO_ROWS
    n_halo = seq // HALO_ROWS
    cur = pl.BlockSpec((1, tc, n_ch), lambda bi, i: (bi, i, 0))
    prev = pl.BlockSpec((1, HALO_ROWS, n_ch),
                        lambda bi, i: (bi, jnp.maximum(i * per - 1, 0), 0))
    nxt = pl.BlockSpec((1, HALO_ROWS, n_ch),
                       lambda bi, i: (bi, jnp.minimum((i + 1) * per, n_halo - 1), 0))
    kern = functools.partial(_conv_kernel, tc=tc, col_block=2 * LANES)
    return pl.pallas_call(
        kern,
        grid=(bsz, seq // tc),
        in_specs=[prev, cur, nxt, _resident(w.shape), _resident(b.shape)],
        out_specs=cur,
        out_shape=jax.ShapeDtypeStruct(xbc.shape, BF16),
        compiler_params=_params("parallel", "parallel"),
        name="conv",
    )(xbc, xbc, xbc, w, b)


def _attn_kernel(q_ref, k_ref, vt_ref, lam_ref, g_ref, o_ref, st_a_ref, st_b_ref, *, lam_init):
    seq = k_ref.shape[1]
    n_q = seq // ATTN_QB
    n_kb = seq // ATTN_KB
    lp = lam_ref[...]
    lam = (jnp.exp(jnp.sum(lp[0:1] * lp[1:2], axis=-1, keepdims=True))
           - jnp.exp(jnp.sum(lp[2:3] * lp[3:4], axis=-1, keepdims=True)) + lam_init)
    first = lax.broadcasted_iota(jnp.int32, (ATTN_QB, ATTN_V_DIM), 1) < ATTN_QK_DIM
    ones_rows = jnp.ones((BF16_ROWS, ATTN_KB), BF16)

    def both_maps(c):
        q = q_ref[0, pl.ds(pl.multiple_of(c * ATTN_QB, ATTN_QB), ATTN_QB), :]
        zero = jnp.zeros_like(q)
        return jnp.concatenate([jnp.where(first, q, zero), jnp.where(first, zero, q)], axis=0)

    def scores_block(q2, st_ref, kb, m8):
        keys = slice(kb * ATTN_KB, (kb + 1) * ATTN_KB)
        st = _dot_nt(k_ref[0, keys, :], q2)
        st_ref[keys, :] = st
        return jnp.maximum(m8, jnp.max(st.reshape(ATTN_KB // 8, 8, 2 * ATTN_QB), axis=0))

    def values_block(st_ref, kb, m, acc):
        keys = slice(kb * ATTN_KB, (kb + 1) * ATTN_KB)
        p = jnp.exp2(st_ref[keys, :] - m).astype(BF16)
        vt_aug = jnp.concatenate([vt_ref[0, :, keys], ones_rows], axis=0)
        return acc + _dot(vt_aug, p)

    def finish(c, acc):
        o1 = acc[:ATTN_V_DIM, :ATTN_QB] * (1.0 / acc[ATTN_V_DIM:ATTN_V_DIM + 1, :ATTN_QB])
        o2 = acc[:ATTN_V_DIM, ATTN_QB:] * (1.0 / acc[ATTN_V_DIM:ATTN_V_DIM + 1, ATTN_QB:])
        o = (o1 - lam * o2).T
        rows = pl.ds(pl.multiple_of(c * ATTN_QB, ATTN_QB), ATTN_QB)
        o_ref[0, rows, :] = (_rms(o, g_ref[...]) * (1.0 - lam_init)).astype(o_ref.dtype)

    m8_init = jnp.full((8, 2 * ATTN_QB), -jnp.inf, F32)
    acc_init = jnp.zeros((ATTN_V_DIM + BF16_ROWS, 2 * ATTN_QB), F32)

    def scores_only(c, st_ref):
        q2 = both_maps(c)
        m8 = m8_init
        for kb in range(n_kb):
            m8 = scores_block(q2, st_ref, kb, m8)
        return jnp.max(m8, axis=0, keepdims=True)

    def values_only(c, st_ref, m):
        acc = acc_init
        for kb in range(n_kb):
            acc = values_block(st_ref, kb, m, acc)
        finish(c, acc)

    def overlapped(c, st_new_ref, st_prev_ref, m_prev):
        q2 = both_maps(c)
        m8, acc = m8_init, acc_init
        for kb in range(n_kb):
            m8 = scores_block(q2, st_new_ref, kb, m8)
            acc = values_block(st_prev_ref, kb, m_prev, acc)
        finish(c - 1, acc)
        return jnp.max(m8, axis=0, keepdims=True)

    m_a = scores_only(0, st_a_ref)

    def two_blocks(i, m_a):
        m_b = overlapped(2 * i + 1, st_b_ref, st_a_ref, m_a)
        return overlapped(2 * i + 2, st_a_ref, st_b_ref, m_b)

    n_pairs = (n_q - 1) // 2
    m_a = lax.fori_loop(0, n_pairs, two_blocks, m_a)
    if (n_q - 1) % 2:
        m_b = overlapped(n_q - 1, st_b_ref, st_a_ref, m_a)
        values_only(n_q - 1, st_b_ref, m_b)
    else:
        values_only(n_q - 1, st_a_ref, m_a)


def _attention(q, k, vt, lam_params, g, lam_init):
    bsz, seq, width = q.shape
    heads = width // ATTN_V_DIM
    qo = pl.BlockSpec((1, seq, ATTN_V_DIM), lambda b, h: (b, 0, h))
    vt_spec = pl.BlockSpec((1, ATTN_V_DIM, seq), lambda b, h: (b, h, 0))
    kern = functools.partial(_attn_kernel, lam_init=lam_init)
    return pl.pallas_call(
        kern,
        grid=(bsz, heads),
        in_specs=[qo, qo, vt_spec, _resident(lam_params.shape), _resident(g.shape)],
        out_specs=qo,
        out_shape=jax.ShapeDtypeStruct(q.shape, BF16),
        scratch_shapes=[pltpu.VMEM((seq, 2 * ATTN_QB), F32), pltpu.VMEM((seq, 2 * ATTN_QB), F32)],
        compiler_params=_params("parallel", "parallel"),
        name="attention",
    )(q, k, vt, lam_params, g)


def _cumsum_rows(x, reverse):
    n = x.shape[0]
    row = lax.broadcasted_iota(jnp.int32, x.shape, 0)
    step = 1
    while step < n:
        if reverse:
            shifted = pltpu.roll(x, n - step, 0)
            x = x + jnp.where(row < n - step, shifted, 0.0)
        else:
            shifted = pltpu.roll(x, step, 0)
            x = x + jnp.where(row >= step, shifted, 0.0)
        step *= 2
    return x


def _ssd_direction(xc_ref, dt_ref, alog_ref, state_ref, y_ref, *, lane0, reverse):
    L = SSD_CHUNK
    n_x = SSD_HEADS * SSD_HEAD_DIM
    gw = SSD_STATE
    dt = dt_ref[0]
    a = -jnp.exp(alog_ref[...])
    cs = _cumsum_rows(dt * a, reverse)
    tot = cs[0:1] if reverse else cs[L - 1:L]
    w = dt * jnp.exp(tot - cs)
    ecs = jnp.exp(cs)
    cs_t = cs.T
    dt_t = dt.T
    w_t = w.T
    row = lax.broadcasted_iota(jnp.int32, (L, L), 0)
    col = lax.broadcasted_iota(jnp.int32, (L, L), 1)
    keep = (row <= col) if reverse else (row >= col)
    lane = lax.broadcasted_iota(jnp.int32, (L, LANES), 1)
    first = lane < SSD_HEAD_DIM

    cb, b_t, c_f = [], [], []
    for g in range(SSD_GROUPS):
        b_g = xc_ref[0, :, n_x + g * gw:n_x + (g + 1) * gw]
        c_g = xc_ref[0, :, n_x + SSD_GROUPS * gw + g * gw:n_x + SSD_GROUPS * gw + (g + 1) * gw]
        cb.append(_dot_nt(c_g, b_g))
        b_t.append(b_g.astype(F32).T)
        c_f.append(c_g.astype(F32))

    heads_per_group = SSD_HEADS // SSD_GROUPS
    for pair in range(SSD_HEADS // 2):
        cols = slice(pair * LANES, (pair + 1) * LANES)
        xs = xc_ref[0, :, cols]
        prev = state_ref[:, cols]
        rhs = jnp.concatenate([xs, prev.astype(BF16)], axis=0)
        ys, ss, cds = [], [], []
        for h in (2 * pair, 2 * pair + 1):
            g = h // heads_per_group
            hl = lane0 + h
            col_cs = jnp.broadcast_to(cs[:, hl:hl + 1], (L, L))
            row_cs = jnp.broadcast_to(cs_t[hl:hl + 1, :], (L, L))
            row_dt = jnp.broadcast_to(dt_t[hl:hl + 1, :], (L, L))
            decay = jnp.where(keep, jnp.exp(col_cs - row_cs), 0.0)
            m = (cb[g] * decay * row_dt).astype(BF16)
            c_e = (c_f[g] * jnp.broadcast_to(ecs[:, hl:hl + 1], (L, gw))).astype(BF16)
            ys.append(_dot(jnp.concatenate([m, c_e], axis=1), rhs))
            b_w = (b_t[g] * jnp.broadcast_to(w_t[hl:hl + 1, :], (gw, L))).astype(BF16)
            ss.append(_dot(b_w, xs))
            cds.append(jnp.broadcast_to(jnp.exp(tot[:, hl:hl + 1]), (1, LANES)))
        first_row = first[0:1]
        y_ref[0, :, cols] = jnp.where(first, ys[0], ys[1]).astype(y_ref.dtype)
        chunk_decay = jnp.where(first_row, cds[0], cds[1])
        state_ref[:, cols] = prev * chunk_decay + jnp.where(first, ss[0], ss[1])


def _ssd_kernel(xf_ref, xb_ref, dtf_ref, dtb_ref, alog_ref, yf_ref, yb_ref, sf_ref, sb_ref):
    @pl.when(pl.program_id(1) == 0)
    def _():
        sf_ref[...] = jnp.zeros_like(sf_ref)
        sb_ref[...] = jnp.zeros_like(sb_ref)

    _ssd_direction(xf_ref, dtf_ref, alog_ref, sf_ref, yf_ref, lane0=0, reverse=False)
    _ssd_direction(xb_ref, dtb_ref, alog_ref, sb_ref, yb_ref, lane0=SSD_HEADS, reverse=True)


def _ssd(xconv, dt, alog):
    bsz, seq, n_ch = xconv.shape
    n_x = SSD_HEADS * SSD_HEAD_DIM
    nc = seq // SSD_CHUNK
    fwd = lambda n: pl.BlockSpec((1, SSD_CHUNK, n), lambda b, t: (b, t, 0))
    bwd = lambda n: pl.BlockSpec((1, SSD_CHUNK, n), lambda b, t: (b, nc - 1 - t, 0))
    y_shape = jax.ShapeDtypeStruct((bsz, seq, n_x), BF16)
    return pl.pallas_call(
        _ssd_kernel,
        grid=(bsz, nc),
        in_specs=[fwd(n_ch), bwd(n_ch), fwd(LANES), bwd(LANES), _resident(alog.shape)],
        out_specs=[fwd(n_x), bwd(n_x)],
        out_shape=[y_shape, y_shape],
        scratch_shapes=[pltpu.VMEM((SSD_STATE, n_x), F32), pltpu.VMEM((SSD_STATE, n_x), F32)],
        compiler_params=_params("parallel", "arbitrary"),
        name="ssd",
    )(xconv, xconv, dt, dt, alog)


def _out_ffn_kernel(h_ref, attn_ref, yf_ref, yb_ref, xs_ref, z_ref, dskip_ref, gssd_ref,
                    wo_ref, gmix_ref, gpre_ref, wg_ref, wu_ref, wd_ref, gpost_ref,
                    gfinal_ref, o_ref):
    n_attn = attn_ref.shape[-1]
    y = (yf_ref[...].astype(F32) + yb_ref[...].astype(F32)
         + xs_ref[...].astype(F32) * dskip_ref[...])
    y = _rms(y * _silu(z_ref[...].astype(F32)), gssd_ref[...])
    m = _dot(attn_ref[...], wo_ref[:n_attn, :]) + _dot(y.astype(BF16), wo_ref[n_attn:, :])
    h = h_ref[...] + _rms(m, gmix_ref[...])
    h = _swiglu_half_step(h, gpre_ref[...], wg_ref, wu_ref, wd_ref, gpost_ref[...])
    o_ref[...] = _rms(h, gfinal_ref[...])


def _out_ffn(h2d, attn, yf, yb, xconv, z, dskip, gssd, wo, gmix, gpre, wg, wu, wd, gpost,
             gfinal, tm):
    t, d = h2d.shape
    f = wg.shape[1]
    n_x = yf.shape[-1]
    row = lambda n: pl.BlockSpec((tm, n), lambda i: (i, 0))
    return pl.pallas_call(
        _out_ffn_kernel,
        grid=(t // tm,),
        in_specs=[row(d), row(attn.shape[-1]), row(n_x), row(n_x), row(n_x), row(n_x),
                  _resident((1, n_x)), _resident((1, n_x)), _resident(wo.shape),
                  _resident((1, d)), _resident((1, d)), _resident((d, f)),
                  _resident((d, f)), _resident((f, d)), _resident((1, d)),
                  _resident((1, d))],
        out_specs=row(d),
        out_shape=jax.ShapeDtypeStruct((t, d), F32),
        compiler_params=_params("parallel"),
        name="out_ffn",
    )(h2d, attn, yf, yb, xconv, z, dskip, gssd, wo, gmix, gpre, wg, wu, wd, gpost, gfinal)


def _rotary_tables(seq):
    half = ROT_DIM // 2
    pos = jnp.arange(seq, dtype=F32)
    inv_freq = jnp.power(F32(ROPE_THETA), -jnp.arange(0, ROT_DIM, 2, dtype=F32) / ROT_DIM)
    ang = pos[:, None] * inv_freq[None, :]
    cos, sin = jnp.cos(ang), jnp.sin(ang)
    ones = jnp.ones((seq, ATTN_QK_DIM - ROT_DIM), F32)
    zeros_h = jnp.zeros((seq, half), F32)
    zeros_r = jnp.zeros((seq, ATTN_QK_DIM - ROT_DIM), F32)
    reps = LANES // ATTN_QK_DIM
    cos_t = jnp.tile(jnp.concatenate([cos, cos, ones], axis=1), (1, reps))
    sa_t = jnp.tile(jnp.concatenate([-sin, zeros_h, zeros_r], axis=1), (1, reps))
    sb_t = jnp.tile(jnp.concatenate([zeros_h, sin, zeros_r], axis=1), (1, reps))
    return cos_t, sa_t, sb_t


def _pad_lanes(v, width=LANES):
    return jnp.pad(v, ((0, 0), (0, width - v.shape[-1])))


def _layer(h2d, layer_idx, p, bsz, seq):
    t, d = h2d.shape
    tm = min(512, t)
    row = lambda v: v.reshape(1, -1).astype(F32)
    bf = lambda w: w.astype(BF16)

    h1 = _ffn(h2d, row(p["ffn1_pre_g"]), bf(p["ffn1_w_gate"]), bf(p["ffn1_w_up"]),
              bf(p["ffn1_w_down"]), row(p["ffn1_post_g"]), tm)

    qk_cols = ATTN_HEADS * 2 * ATTN_QK_DIM
    v_cols = ATTN_HEADS * ATTN_V_DIM
    n_x = SSD_HEADS * SSD_HEAD_DIM
    xbc_cols = n_x + 2 * SSD_GROUPS * SSD_STATE
    w_in = p["w_in"]
    v0 = 2 * qk_cols
    wvt = bf(w_in[:, v0:v0 + v_cols].T)
    n_dt = w_in.shape[1] - (v0 + v_cols + n_x + xbc_cols)
    w_pad = bf(jnp.pad(jnp.concatenate([w_in[:, :v0], w_in[:, v0 + v_cols:]], axis=1),
                       ((0, 0), (0, LANES - n_dt))))
    dt_bias = _pad_lanes(jnp.concatenate([p["dt_bias_fwd"], p["dt_bias_bwd"]]).reshape(1, -1)
                         .astype(F32))
    alog = _pad_lanes(jnp.concatenate([p["a_log_fwd"], p["a_log_bwd"]]).reshape(1, -1)
                      .astype(F32))
    cos_t, sa_t, sb_t = _rotary_tables(seq)
    tp = min(tm, seq)
    q, k, vt, z, xbc, dt = _in_proj(h1, row(p["mix_pre_g"]), w_pad, wvt, cos_t, sa_t, sb_t,
                                    dt_bias, tp, seq, qk_cols, n_x, xbc_cols)

    shape3 = lambda a: a.reshape(bsz, seq, a.shape[-1])
    xconv = _conv(shape3(xbc), p["conv_w"].astype(F32), row(p["conv_b"]), min(512, seq))

    lam_init = 0.8 - 0.6 * math.exp(-0.3 * layer_idx)
    lam_params = jnp.stack([p["lambda_q1"], p["lambda_k1"], p["lambda_q2"],
                            p["lambda_k2"]]).astype(F32)
    attn = _attention(shape3(q), shape3(k), vt, lam_params, row(p["attn_subln_g"]), lam_init)

    yf, yb = _ssd(xconv, shape3(dt), alog)

    dskip = jnp.repeat(p["d_skip"].astype(F32), SSD_HEAD_DIM).reshape(1, -1)
    flat = lambda a: a.reshape(t, a.shape[-1])
    return _out_ffn(h1, flat(attn), flat(yf), flat(yb), flat(xconv), z, dskip,
                    row(p["ssd_norm_g"]), bf(p["w_out"]), row(p["mix_post_g"]),
                    row(p["ffn2_pre_g"]), bf(p["ffn2_w_gate"]), bf(p["ffn2_w_up"]),
                    bf(p["ffn2_w_down"]), row(p["ffn2_post_g"]), row(p["final_g"]), tm)


def kernel(x, ffn1_pre_g, ffn1_w_gate, ffn1_w_up, ffn1_w_down, ffn1_post_g, mix_pre_g, w_in, lambda_q1, lambda_k1, lambda_q2, lambda_k2, attn_subln_g, conv_w, conv_b, a_log_fwd, a_log_bwd, dt_bias_fwd, dt_bias_bwd, d_skip, ssd_norm_g, w_out, mix_post_g, ffn2_pre_g, ffn2_w_gate, ffn2_w_up, ffn2_w_down, ffn2_post_g, final_g):
    names = ("ffn1_pre_g", "ffn1_w_gate", "ffn1_w_up", "ffn1_w_down", "ffn1_post_g",
             "mix_pre_g", "w_in", "lambda_q1", "lambda_k1", "lambda_q2", "lambda_k2",
             "attn_subln_g", "conv_w", "conv_b", "a_log_fwd", "a_log_bwd", "dt_bias_fwd",
             "dt_bias_bwd", "d_skip", "ssd_norm_g", "w_out", "mix_post_g", "ffn2_pre_g",
             "ffn2_w_gate", "ffn2_w_up", "ffn2_w_down", "ffn2_post_g", "final_g")
    stacked = dict(zip(names, (ffn1_pre_g, ffn1_w_gate, ffn1_w_up, ffn1_w_down, ffn1_post_g,
                               mix_pre_g, w_in, lambda_q1, lambda_k1, lambda_q2, lambda_k2,
                               attn_subln_g, conv_w, conv_b, a_log_fwd, a_log_bwd,
                               dt_bias_fwd, dt_bias_bwd, d_skip, ssd_norm_g, w_out,
                               mix_post_g, ffn2_pre_g, ffn2_w_gate, ffn2_w_up, ffn2_w_down,
                               ffn2_post_g, final_g)))
    bsz, seq, d = x.shape
    h = x.reshape(bsz * seq, d)
    for i in range(ffn1_pre_g.shape[0]):
        h = _layer(h, i, {n: a[i] for n, a in stacked.items()}, bsz, seq)
    return h.reshape(bsz, seq, d)
```

```python
import functools
import math

import jax
import jax.numpy as jnp
from jax import lax
from jax.experimental import pallas as pl
from jax.experimental.pallas import tpu as pltpu

F32 = jnp.float32
BF16 = jnp.bfloat16

EPS = 1e-6
ATTN_HEADS = 8
ATTN_QK_DIM = 64
ATTN_V_DIM = 2 * ATTN_QK_DIM
ROT_DIM = ATTN_QK_DIM // 4
ROPE_THETA = 500000.0
SSD_HEADS = 16
SSD_HEAD_DIM = 64
SSD_GROUPS = 2
SSD_STATE = 128
SSD_CONV = 5
SSD_CHUNK = 128

LANES = 128
MXU_COLS = 256
BF16_ROWS = 16
LOG2_E = math.log2(math.e)
ATTN_QB = MXU_COLS
ATTN_KB = MXU_COLS
ATTN_STEPS_PER_TRIP = 8
ROW_TILE = 512
CONV_ROWS = 64
SUBLANES = 8
VMEM_LIMIT_BYTES = 56 * 1024 * 1024


def _params(*semantics):
    return pltpu.CompilerParams(dimension_semantics=semantics,
                                vmem_limit_bytes=VMEM_LIMIT_BYTES)


def _resident(shape):
    zeros = (0,) * len(shape)
    return pl.BlockSpec(shape, lambda *_: zeros, pipeline_mode=pl.Buffered(1))


def _rms(x, g):
    return x * lax.rsqrt(jnp.mean(x * x, axis=-1, keepdims=True) + EPS) * g


def _silu(x):
    return x * jax.nn.sigmoid(x)


def _dot(a, b):
    return jnp.dot(a, b, preferred_element_type=F32)


def _dot_nt(a, b):
    return lax.dot_general(a, b, (((1,), (1,)), ((), ())), preferred_element_type=F32)


def _swiglu_half_step(x, gpre, wg_ref, wu_ref, wd_ref, gpost):
    xn = _rms(x, gpre).astype(BF16)
    g = _dot(xn, wg_ref[...])
    u = _dot(xn, wu_ref[...])
    a = (_silu(g) * u).astype(BF16)
    f = _dot(a, wd_ref[...])
    return x + 0.5 * _rms(f, gpost)


def _ffn_kernel(x_ref, gpre_ref, wg_ref, wu_ref, wd_ref, gpost_ref, o_ref):
    o_ref[...] = _swiglu_half_step(x_ref[...], gpre_ref[...], wg_ref, wu_ref, wd_ref,
                                   gpost_ref[...])


def _ffn(x2d, gpre, wg, wu, wd, gpost, tm):
    t, d = x2d.shape
    f = wg.shape[1]
    row = pl.BlockSpec((tm, d), lambda i: (i, 0))
    return pl.pallas_call(
        _ffn_kernel,
        grid=(t // tm,),
        in_specs=[row, _resident((1, d)), _resident((d, f)), _resident((d, f)),
                  _resident((f, d)), _resident((1, d))],
        out_specs=row,
        out_shape=jax.ShapeDtypeStruct((t, d), F32),
        compiler_params=_params("parallel"),
        name="ffn1",
    )(x2d, gpre, wg, wu, wd, gpost)


def _in_proj_kernel(hp_ref, h_ref, hx_ref, g_ref, w_ref, wvt_ref, cos_ref, sa_ref, sb_ref,
                    dtb_ref, cw_ref, cb_ref, q_ref, k_ref, vt_ref, z_ref, xc_ref, dt_ref, *,
                    qk_cols, z_cols, xbc_cols, nb):
    tm = h_ref.shape[0]
    hn = _rms(h_ref[...], g_ref[...]).astype(BF16)
    hn_ext = jnp.concatenate([_rms(hp_ref[...], g_ref[...]).astype(BF16), hn,
                              _rms(hx_ref[...], g_ref[...]).astype(BF16)], axis=0)
    i = pl.program_id(0) % nb
    row = lax.broadcasted_iota(jnp.int32, (tm + 2 * SUBLANES, 1), 0)
    inside = jnp.logical_and(jnp.logical_or(i > 0, row >= SUBLANES),
                             jnp.logical_or(i < nb - 1, row < tm + SUBLANES))
    pad = SSD_CONV // 2
    c_z = 2 * qk_cols
    c_x = c_z + z_cols
    c_dt = c_x + xbc_cols
    cos = cos_ref[...]
    sa = sa_ref[...]
    sb = sb_ref[...]

    def conv_dot(c):
        ext = _dot(hn_ext, w_ref[:, c_x + c * MXU_COLS:c_x + (c + 1) * MXU_COLS])
        return jnp.where(inside, ext, 0.0)

    def conv_taps(c, ext):
        cols = slice(c * MXU_COLS, (c + 1) * MXU_COLS)
        for r0 in range(0, tm, CONV_ROWS):
            piece = ext[r0:r0 + CONV_ROWS + 2 * SUBLANES]
            acc = jnp.zeros((CONV_ROWS, MXU_COLS), F32) + cb_ref[:, cols]
            for k in range(SSD_CONV):
                off = SUBLANES - pad + k
                acc = acc + piece[off:off + CONV_ROWS] * cw_ref[k:k + 1, cols]
            xc_ref[r0:r0 + CONV_ROWS, cols] = _silu(acc).astype(xc_ref.dtype)

    def rotary_block(col0, out_ref, mult, c):
        t2 = _dot(hn, w_ref[:, col0 + c * MXU_COLS:col0 + (c + 1) * MXU_COLS])
        for half in range(MXU_COLS // LANES):
            t = t2[:, half * LANES:(half + 1) * LANES]
            r = t * cos + pltpu.roll(t, LANES - ROT_DIM // 2, 1) * sa \
                + pltpu.roll(t, ROT_DIM // 2, 1) * sb
            out_ref[0, c * (MXU_COLS // LANES) + half] = (r * mult).astype(out_ref.dtype)

    def z_block(c):
        cols = slice(c * MXU_COLS, (c + 1) * MXU_COLS)
        z_ref[:, cols] = _dot(hn, w_ref[:, c_z + c * MXU_COLS:c_z + (c + 1) * MXU_COLS]
                              ).astype(z_ref.dtype)

    n_conv = xbc_cols // MXU_COLS
    n_rot = qk_cols // MXU_COLS
    units = ([functools.partial(rotary_block, 0, q_ref, ATTN_QK_DIM ** -0.5 * LOG2_E, c)
              for c in range(n_rot)]
             + [functools.partial(rotary_block, qk_cols, k_ref, 1.0, c) for c in range(n_rot)]
             + [functools.partial(z_block, c) for c in range(z_cols // MXU_COLS)])
    per_conv = -(-len(units) // n_conv)
    for c in range(n_conv):
        ext = conv_dot(c)
        for unit in units[c * per_conv:(c + 1) * per_conv]:
            unit()
        conv_taps(c, ext)
    vt_ref[0] = _dot_nt(wvt_ref[...], hn).astype(vt_ref.dtype).reshape(vt_ref.shape[1:])
    dt_raw = _dot(hn, w_ref[:, c_dt:c_dt + LANES]) + dtb_ref[...]
    dt_ref[...] = jax.nn.softplus(dt_raw)


def _in_proj(h2d, g, w_pad, wvt, cos, sa, sb, dtb, conv_w, conv_b, tm, seq, qk_cols, z_cols,
             xbc_cols):
    t, d = h2d.shape
    per = tm // SUBLANES
    n_halo = t // SUBLANES
    prev = pl.BlockSpec((SUBLANES, d), lambda i: (jnp.maximum(i * per - 1, 0), 0))
    nxt = pl.BlockSpec((SUBLANES, d), lambda i: (jnp.minimum((i + 1) * per, n_halo - 1), 0))
    v_cols = wvt.shape[0]
    nb = seq // tm
    row = lambda n: pl.BlockSpec((tm, n), lambda i: (i, 0))
    pos = pl.BlockSpec((tm, LANES), lambda i: (i % nb, 0))
    heads = qk_cols // LANES
    qk_spec = pl.BlockSpec((1, heads, tm, LANES), lambda i: (i // nb, 0, i % nb, 0))
    vt_spec = pl.BlockSpec((1, heads, v_cols // heads, tm), lambda i: (i // nb, 0, 0, i % nb))
    kern = functools.partial(_in_proj_kernel, qk_cols=qk_cols, z_cols=z_cols,
                             xbc_cols=xbc_cols, nb=nb)
    return pl.pallas_call(
        kern,
        grid=(t // tm,),
        in_specs=[prev, row(d), nxt, _resident((1, d)), _resident(w_pad.shape),
                  _resident(wvt.shape), pos, pos, pos, _resident((1, LANES)),
                  _resident(conv_w.shape), _resident(conv_b.shape)],
        out_specs=[qk_spec, qk_spec, vt_spec, row(z_cols), row(xbc_cols), row(LANES)],
        out_shape=[jax.ShapeDtypeStruct((t // seq, heads, seq, LANES), BF16),
                   jax.ShapeDtypeStruct((t // seq, heads, seq, LANES), BF16),
                   jax.ShapeDtypeStruct((t // seq, heads, v_cols // heads, seq), BF16),
                   jax.ShapeDtypeStruct((t, z_cols), BF16),
                   jax.ShapeDtypeStruct((t, xbc_cols), BF16),
                   jax.ShapeDtypeStruct((t, LANES), F32)],
        compiler_params=_params("parallel"),
        name="in_proj",
    )(h2d, h2d, h2d, g, w_pad, wvt, cos, sa, sb, dtb, conv_w, conv_b)


def _attn_kernel(q_ref, k_ref, vt_ref, lam_ref, g_ref, o_ref, st_a_ref, st_b_ref, *, lam_init):
    heads, seq = k_ref.shape[1], k_ref.shape[2]
    n_q = seq // ATTN_QB
    n_kb = seq // ATTN_KB
    n_steps = heads * n_q
    lp = lam_ref[...]
    lam = (jnp.exp(jnp.sum(lp[0:1] * lp[1:2], axis=-1, keepdims=True))
           - jnp.exp(jnp.sum(lp[2:3] * lp[3:4], axis=-1, keepdims=True)) + lam_init)
    first = lax.broadcasted_iota(jnp.int32, (ATTN_QB, ATTN_V_DIM), 1) < ATTN_QK_DIM
    ones_rows = jnp.ones((BF16_ROWS, ATTN_KB), BF16)

    def query_rows(c):
        return pl.ds(pl.multiple_of((c % n_q) * ATTN_QB, ATTN_QB), ATTN_QB)

    def both_maps(c):
        q = q_ref[0, c // n_q, query_rows(c), :]
        zero = jnp.zeros_like(q)
        return jnp.concatenate([jnp.where(first, q, zero), jnp.where(first, zero, q)], axis=0)

    def scores_block(c, q2, st_ref, kb, m8):
        keys = slice(kb * ATTN_KB, (kb + 1) * ATTN_KB)
        st = _dot_nt(k_ref[0, c // n_q, keys, :], q2)
        st_ref[keys, :] = st
        return jnp.maximum(m8, jnp.max(st.reshape(ATTN_KB // 8, 8, 2 * ATTN_QB), axis=0))

    def values_block(c, st_ref, kb, m, acc):
        keys = slice(kb * ATTN_KB, (kb + 1) * ATTN_KB)
        p = jnp.exp2(st_ref[keys, :] - m).astype(BF16)
        vt_aug = jnp.concatenate([vt_ref[0, c // n_q, :, keys], ones_rows], axis=0)
        return acc + _dot(vt_aug, p)

    def finish(c, acc):
        o1 = acc[:ATTN_V_DIM, :ATTN_QB] * (1.0 / acc[ATTN_V_DIM:ATTN_V_DIM + 1, :ATTN_QB])
        o2 = acc[:ATTN_V_DIM, ATTN_QB:] * (1.0 / acc[ATTN_V_DIM:ATTN_V_DIM + 1, ATTN_QB:])
        o = (o1 - lam * o2).T
        o_ref[0, c // n_q, query_rows(c), :] = (
            _rms(o, g_ref[...]) * (1.0 - lam_init)).astype(o_ref.dtype)

    m8_init = jnp.full((8, 2 * ATTN_QB), -jnp.inf, F32)
    acc_init = jnp.zeros((ATTN_V_DIM + BF16_ROWS, 2 * ATTN_QB), F32)
    buffers = (st_a_ref, st_b_ref)

    def scores_only(c):
        q2 = both_maps(c)
        m8 = m8_init
        for kb in range(n_kb):
            m8 = scores_block(c, q2, buffers[0], kb, m8)
        return jnp.max(m8, axis=0, keepdims=True)

    def values_only(c, parity, m):
        acc = acc_init
        for kb in range(n_kb):
            acc = values_block(c, buffers[parity], kb, m, acc)
        finish(c, acc)

    def overlapped(c, parity, m_prev):
        q2 = both_maps(c)
        m8, acc = m8_init, acc_init
        for kb in range(n_kb):
            m8 = scores_block(c, q2, buffers[parity], kb, m8)
            acc = values_block(c - 1, buffers[1 - parity], kb, m_prev, acc)
        finish(c - 1, acc)
        return jnp.max(m8, axis=0, keepdims=True)

    m = scores_only(0)
    n_trips = (n_steps - 1) // ATTN_STEPS_PER_TRIP

    def trip(i, m):
        for j in range(ATTN_STEPS_PER_TRIP):
            m = overlapped(i * ATTN_STEPS_PER_TRIP + 1 + j, (1 + j) % 2, m)
        return m

    m = lax.fori_loop(0, n_trips, trip, m)
    for c in range(n_trips * ATTN_STEPS_PER_TRIP + 1, n_steps):
        m = overlapped(c, c % 2, m)
    values_only(n_steps - 1, (n_steps - 1) % 2, m)


def _attention(q, k, vt, lam_params, g, lam_init):
    bsz, heads, seq, _ = q.shape
    qo = pl.BlockSpec((1, heads, seq, ATTN_V_DIM), lambda b: (b, 0, 0, 0))
    vt_spec = pl.BlockSpec((1, heads, ATTN_V_DIM, seq), lambda b: (b, 0, 0, 0))
    kern = functools.partial(_attn_kernel, lam_init=lam_init)
    return pl.pallas_call(
        kern,
        grid=(bsz,),
        in_specs=[qo, qo, vt_spec, _resident(lam_params.shape), _resident(g.shape)],
        out_specs=qo,
        out_shape=jax.ShapeDtypeStruct(q.shape, BF16),
        scratch_shapes=[pltpu.VMEM((seq, 2 * ATTN_QB), F32), pltpu.VMEM((seq, 2 * ATTN_QB), F32)],
        compiler_params=_params("parallel"),
        name="attention",
    )(q, k, vt, lam_params, g)


def _cumsum_rows(x, reverse):
    n = x.shape[0]
    row = lax.broadcasted_iota(jnp.int32, x.shape, 0)
    step = 1
    while step < n:
        if reverse:
            shifted = pltpu.roll(x, n - step, 0)
            x = x + jnp.where(row < n - step, shifted, 0.0)
        else:
            shifted = pltpu.roll(x, step, 0)
            x = x + jnp.where(row >= step, shifted, 0.0)
        step *= 2
    return x


def _ssd_direction(xc_ref, dt_ref, alog_ref, state_ref, y_ref, *, lane0, reverse):
    L = SSD_CHUNK
    n_x = SSD_HEADS * SSD_HEAD_DIM
    gw = SSD_STATE
    dt = dt_ref[0]
    a = -jnp.exp(alog_ref[...])
    cs = _cumsum_rows(dt * a, reverse)
    tot = cs[0:1] if reverse else cs[L - 1:L]
    w = dt * jnp.exp(tot - cs)
    ecs = jnp.exp(cs)
    cs_t = cs.T
    dt_t = dt.T
    w_t = w.T
    row = lax.broadcasted_iota(jnp.int32, (L, L), 0)
    col = lax.broadcasted_iota(jnp.int32, (L, L), 1)
    keep = (row <= col) if reverse else (row >= col)
    lane = lax.broadcasted_iota(jnp.int32, (L, LANES), 1)
    first = lane < SSD_HEAD_DIM

    cb, b_t, c_f = [], [], []
    for g in range(SSD_GROUPS):
        b_g = xc_ref[0, :, n_x + g * gw:n_x + (g + 1) * gw]
        c_g = xc_ref[0, :, n_x + SSD_GROUPS * gw + g * gw:n_x + SSD_GROUPS * gw + (g + 1) * gw]
        cb.append(_dot_nt(c_g, b_g))
        b_t.append(b_g.astype(F32).T)
        c_f.append(c_g.astype(F32))

    heads_per_group = SSD_HEADS // SSD_GROUPS
    for pair in range(SSD_HEADS // 2):
        cols = slice(pair * LANES, (pair + 1) * LANES)
        xs = xc_ref[0, :, cols]
        prev = state_ref[:, cols]
        rhs = jnp.concatenate([xs, prev.astype(BF16)], axis=0)
        ys, ss, cds = [], [], []
        for h in (2 * pair, 2 * pair + 1):
            g = h // heads_per_group
            hl = lane0 + h
            col_cs = jnp.broadcast_to(cs[:, hl:hl + 1], (L, L))
            row_cs = jnp.broadcast_to(cs_t[hl:hl + 1, :], (L, L))
            row_dt = jnp.broadcast_to(dt_t[hl:hl + 1, :], (L, L))
            decay = jnp.where(keep, jnp.exp(col_cs - row_cs), 0.0)
            m = (cb[g] * decay * row_dt).astype(BF16)
            c_e = (c_f[g] * jnp.broadcast_to(ecs[:, hl:hl + 1], (L, gw))).astype(BF16)
            ys.append(_dot(jnp.concatenate([m, c_e], axis=1), rhs))
            b_w = (b_t[g] * jnp.broadcast_to(w_t[hl:hl + 1, :], (gw, L))).astype(BF16)
            ss.append(_dot(b_w, xs))
            cds.append(jnp.broadcast_to(jnp.exp(tot[:, hl:hl + 1]), (1, LANES)))
        first_row = first[0:1]
        y_ref[0, :, cols] = jnp.where(first, ys[0], ys[1]).astype(y_ref.dtype)
        chunk_decay = jnp.where(first_row, cds[0], cds[1])
        state_ref[:, cols] = prev * chunk_decay + jnp.where(first, ss[0], ss[1])


def _ssd_kernel(xf_ref, xb_ref, dtf_ref, dtb_ref, alog_ref, yf_ref, yb_ref, sf_ref, sb_ref):
    @pl.when(pl.program_id(1) == 0)
    def _():
        sf_ref[...] = jnp.zeros_like(sf_ref)
        sb_ref[...] = jnp.zeros_like(sb_ref)

    _ssd_direction(xf_ref, dtf_ref, alog_ref, sf_ref, yf_ref, lane0=0, reverse=False)
    _ssd_direction(xb_ref, dtb_ref, alog_ref, sb_ref, yb_ref, lane0=SSD_HEADS, reverse=True)


def _ssd(xconv, dt, alog):
    bsz, seq, n_ch = xconv.shape
    n_x = SSD_HEADS * SSD_HEAD_DIM
    nc = seq // SSD_CHUNK
    fwd = lambda n: pl.BlockSpec((1, SSD_CHUNK, n), lambda b, t: (b, t, 0))
    bwd = lambda n: pl.BlockSpec((1, SSD_CHUNK, n), lambda b, t: (b, nc - 1 - t, 0))
    y_shape = jax.ShapeDtypeStruct((bsz, seq, n_x), BF16)
    return pl.pallas_call(
        _ssd_kernel,
        grid=(bsz, nc),
        in_specs=[fwd(n_ch), bwd(n_ch), fwd(LANES), bwd(LANES), _resident(alog.shape)],
        out_specs=[fwd(n_x), bwd(n_x)],
        out_shape=[y_shape, y_shape],
        scratch_shapes=[pltpu.VMEM((SSD_STATE, n_x), F32), pltpu.VMEM((SSD_STATE, n_x), F32)],
        compiler_params=_params("parallel", "arbitrary"),
        name="ssd",
    )(xconv, xconv, dt, dt, alog)


def _out_ffn_kernel(h_ref, attn_ref, yf_ref, yb_ref, xs_ref, z_ref, dskip_ref, gssd_ref,
                    wo_ref, gmix_ref, gpre_ref, wg_ref, wu_ref, wd_ref, gpost_ref,
                    gfinal_ref, o_ref):
    heads = attn_ref.shape[1]
    attn = jnp.concatenate([attn_ref[0, h] for h in range(heads)], axis=1)
    n_attn = attn.shape[-1]
    y = (yf_ref[...].astype(F32) + yb_ref[...].astype(F32)
         + xs_ref[...].astype(F32) * dskip_ref[...])
    y = _rms(y * _silu(z_ref[...].astype(F32)), gssd_ref[...])
    m = _dot(attn, wo_ref[:n_attn, :]) + _dot(y.astype(BF16), wo_ref[n_attn:, :])
    h = h_ref[...] + _rms(m, gmix_ref[...])
    h = _swiglu_half_step(h, gpre_ref[...], wg_ref, wu_ref, wd_ref, gpost_ref[...])
    o_ref[...] = _rms(h, gfinal_ref[...])


def _out_ffn(h2d, attn, yf, yb, xconv, z, dskip, gssd, wo, gmix, gpre, wg, wu, wd, gpost,
             gfinal, tm):
    t, d = h2d.shape
    _, heads, seq, v_dim = attn.shape
    nb = seq // tm
    attn_spec = pl.BlockSpec((1, heads, tm, v_dim), lambda i: (i // nb, 0, i % nb, 0))
    f = wg.shape[1]
    n_x = yf.shape[-1]
    row = lambda n: pl.BlockSpec((tm, n), lambda i: (i, 0))
    return pl.pallas_call(
        _out_ffn_kernel,
        grid=(t // tm,),
        in_specs=[row(d), attn_spec, row(n_x), row(n_x), row(n_x), row(n_x),
                  _resident((1, n_x)), _resident((1, n_x)), _resident(wo.shape),
                  _resident((1, d)), _resident((1, d)), _resident((d, f)),
                  _resident((d, f)), _resident((f, d)), _resident((1, d)),
                  _resident((1, d))],
        out_specs=row(d),
        out_shape=jax.ShapeDtypeStruct((t, d), F32),
        compiler_params=_params("parallel"),
        name="out_ffn",
    )(h2d, attn, yf, yb, xconv, z, dskip, gssd, wo, gmix, gpre, wg, wu, wd, gpost, gfinal)


def _rotary_tables(seq):
    half = ROT_DIM // 2
    pos = jnp.arange(seq, dtype=F32)
    inv_freq = jnp.power(F32(ROPE_THETA), -jnp.arange(0, ROT_DIM, 2, dtype=F32) / ROT_DIM)
    ang = pos[:, None] * inv_freq[None, :]
    cos, sin = jnp.cos(ang), jnp.sin(ang)
    ones = jnp.ones((seq, ATTN_QK_DIM - ROT_DIM), F32)
    zeros_h = jnp.zeros((seq, half), F32)
    zeros_r = jnp.zeros((seq, ATTN_QK_DIM - ROT_DIM), F32)
    reps = LANES // ATTN_QK_DIM
    cos_t = jnp.tile(jnp.concatenate([cos, cos, ones], axis=1), (1, reps))
    sa_t = jnp.tile(jnp.concatenate([-sin, zeros_h, zeros_r], axis=1), (1, reps))
    sb_t = jnp.tile(jnp.concatenate([zeros_h, sin, zeros_r], axis=1), (1, reps))
    return cos_t, sa_t, sb_t


def _pad_lanes(v, width=LANES):
    return jnp.pad(v, ((0, 0), (0, width - v.shape[-1])))


def _layer(h2d, layer_idx, p, bsz, seq):
    t, d = h2d.shape
    tm = min(ROW_TILE, seq)
    row = lambda v: v.reshape(1, -1).astype(F32)
    bf = lambda w: w.astype(BF16)

    h1 = _ffn(h2d, row(p["ffn1_pre_g"]), bf(p["ffn1_w_gate"]), bf(p["ffn1_w_up"]),
              bf(p["ffn1_w_down"]), row(p["ffn1_post_g"]), tm)

    qk_cols = ATTN_HEADS * 2 * ATTN_QK_DIM
    v_cols = ATTN_HEADS * ATTN_V_DIM
    n_x = SSD_HEADS * SSD_HEAD_DIM
    xbc_cols = n_x + 2 * SSD_GROUPS * SSD_STATE
    w_in = p["w_in"]
    v0 = 2 * qk_cols
    wvt = bf(w_in[:, v0:v0 + v_cols].T)
    n_dt = w_in.shape[1] - (v0 + v_cols + n_x + xbc_cols)
    w_pad = bf(jnp.pad(jnp.concatenate([w_in[:, :v0], w_in[:, v0 + v_cols:]], axis=1),
                       ((0, 0), (0, LANES - n_dt))))
    dt_bias = _pad_lanes(jnp.concatenate([p["dt_bias_fwd"], p["dt_bias_bwd"]]).reshape(1, -1)
                         .astype(F32))
    alog = _pad_lanes(jnp.concatenate([p["a_log_fwd"], p["a_log_bwd"]]).reshape(1, -1)
                      .astype(F32))
    cos_t, sa_t, sb_t = _rotary_tables(seq)
    q, k, vt, z, xconv, dt = _in_proj(h1, row(p["mix_pre_g"]), w_pad, wvt, cos_t, sa_t, sb_t,
                                      dt_bias, p["conv_w"].astype(F32), row(p["conv_b"]), tm, seq,
                                      qk_cols, n_x, xbc_cols)
    shape3 = lambda a: a.reshape(bsz, seq, a.shape[-1])
    xconv = shape3(xconv)

    lam_init = 0.8 - 0.6 * math.exp(-0.3 * layer_idx)
    lam_params = jnp.stack([p["lambda_q1"], p["lambda_k1"], p["lambda_q2"],
                            p["lambda_k2"]]).astype(F32)
    attn = _attention(q, k, vt, lam_params, row(p["attn_subln_g"]), lam_init)

    yf, yb = _ssd(xconv, shape3(dt), alog)

    dskip = jnp.repeat(p["d_skip"].astype(F32), SSD_HEAD_DIM).reshape(1, -1)
    flat = lambda a: a.reshape(t, a.shape[-1])
    return _out_ffn(h1, attn, flat(yf), flat(yb), flat(xconv), z, dskip,
                    row(p["ssd_norm_g"]), bf(p["w_out"]), row(p["mix_post_g"]),
                    row(p["ffn2_pre_g"]), bf(p["ffn2_w_gate"]), bf(p["ffn2_w_up"]),
                    bf(p["ffn2_w_down"]), row(p["ffn2_post_g"]), row(p["final_g"]), tm)


def kernel(x, ffn1_pre_g, ffn1_w_gate, ffn1_w_up, ffn1_w_down, ffn1_post_g, mix_pre_g, w_in, lambda_q1, lambda_k1, lambda_q2, lambda_k2, attn_subln_g, conv_w, conv_b, a_log_fwd, a_log_bwd, dt_bias_fwd, dt_bias_bwd, d_skip, ssd_norm_g, w_out, mix_post_g, ffn2_pre_g, ffn2_w_gate, ffn2_w_up, ffn2_w_down, ffn2_post_g, final_g):
    names = ("ffn1_pre_g", "ffn1_w_gate", "ffn1_w_up", "ffn1_w_down", "ffn1_post_g",
             "mix_pre_g", "w_in", "lambda_q1", "lambda_k1", "lambda_q2", "lambda_k2",
             "attn_subln_g", "conv_w", "conv_b", "a_log_fwd", "a_log_bwd", "dt_bias_fwd",
             "dt_bias_bwd", "d_skip", "ssd_norm_g", "w_out", "mix_post_g", "ffn2_pre_g",
             "ffn2_w_gate", "ffn2_w_up", "ffn2_w_down", "ffn2_post_g", "final_g")
    stacked = dict(zip(names, (ffn1_pre_g, ffn1_w_gate, ffn1_w_up, ffn1_w_down, ffn1_post_g,
                               mix_pre_g, w_in, lambda_q1, lambda_k1, lambda_q2, lambda_k2,
                               attn_subln_g, conv_w, conv_b, a_log_fwd, a_log_bwd,
                               dt_bias_fwd, dt_bias_bwd, d_skip, ssd_norm_g, w_out,
                               mix_post_g, ffn2_pre_g, ffn2_w_gate, ffn2_w_up, ffn2_w_down,
                               ffn2_post_g, final_g)))
    bsz, seq, d = x.shape
    h = x.reshape(bsz * seq, d)
    for i in range(ffn1_pre_g.shape[0]):
        h = _layer(h, i, {n: a[i] for n, a in stacked.items()}, bsz, seq)
    return h.reshape(bsz, seq, d)
```

```python
import functools
import math

import jax
import jax.numpy as jnp
from jax import lax
from jax.experimental import pallas as pl
from jax.experimental.pallas import tpu as pltpu

F32 = jnp.float32
BF16 = jnp.bfloat16

EPS = 1e-6
ATTN_HEADS = 8
ATTN_QK_DIM = 64
ATTN_V_DIM = 2 * ATTN_QK_DIM
ROT_DIM = ATTN_QK_DIM // 4
ROPE_THETA = 500000.0
SSD_HEADS = 16
SSD_HEAD_DIM = 64
SSD_GROUPS = 2
SSD_STATE = 128
SSD_CONV = 5
SSD_CHUNK = 128

LANES = 128
MXU_COLS = 256
BF16_ROWS = 16
LOG2_E = math.log2(math.e)
ATTN_QB = MXU_COLS
ATTN_KB = MXU_COLS
ATTN_STEPS_PER_TRIP = 8
ROW_TILE = 512
FFN_ROW_GROUPS = 2
CONV_ROWS = 64
SUBLANES = 8
VMEM_LIMIT_BYTES = 56 * 1024 * 1024


def _params(*semantics):
    return pltpu.CompilerParams(dimension_semantics=semantics,
                                vmem_limit_bytes=VMEM_LIMIT_BYTES)


def _resident(shape):
    zeros = (0,) * len(shape)
    return pl.BlockSpec(shape, lambda *_: zeros, pipeline_mode=pl.Buffered(1))


def _rms(x, g):
    return x * lax.rsqrt(jnp.mean(x * x, axis=-1, keepdims=True) + EPS) * g


def _silu(x):
    return x * jax.nn.sigmoid(x)


def _dot(a, b):
    return jnp.dot(a, b, preferred_element_type=F32)


def _dot_nt(a, b):
    return lax.dot_general(a, b, (((1,), (1,)), ((), ())), preferred_element_type=F32)


def _swiglu_half_step(x, gpre, wg_ref, wu_ref, wd_ref, gpost):
    outs = []
    rows = x.shape[0] // FFN_ROW_GROUPS
    xn = _rms(x, gpre).astype(BF16)
    acts = []
    for r in range(FFN_ROW_GROUPS):
        xr = xn[r * rows:(r + 1) * rows]
        acts.append((_silu(_dot(xr, wg_ref[...])) * _dot(xr, wu_ref[...])).astype(BF16))
    for r in range(FFN_ROW_GROUPS):
        f = _dot(acts[r], wd_ref[...])
        outs.append(x[r * rows:(r + 1) * rows] + 0.5 * _rms(f, gpost))
    return jnp.concatenate(outs, axis=0)


def _ffn_kernel(x_ref, gpre_ref, wg_ref, wu_ref, wd_ref, gpost_ref, o_ref):
    o_ref[...] = _swiglu_half_step(x_ref[...], gpre_ref[...], wg_ref, wu_ref, wd_ref,
                                   gpost_ref[...])


def _ffn(x2d, gpre, wg, wu, wd, gpost, tm):
    t, d = x2d.shape
    f = wg.shape[1]
    row = pl.BlockSpec((tm, d), lambda i: (i, 0))
    return pl.pallas_call(
        _ffn_kernel,
        grid=(t // tm,),
        in_specs=[row, _resident((1, d)), _resident((d, f)), _resident((d, f)),
                  _resident((f, d)), _resident((1, d))],
        out_specs=row,
        out_shape=jax.ShapeDtypeStruct((t, d), F32),
        compiler_params=_params("parallel"),
        name="ffn1",
    )(x2d, gpre, wg, wu, wd, gpost)


def _in_proj_kernel(hp_ref, h_ref, hx_ref, g_ref, w_ref, wvt_ref, cos_ref, sa_ref, sb_ref,
                    dtb_ref, cw_ref, cb_ref, q_ref, k_ref, vt_ref, z_ref, xc_ref, dt_ref, *,
                    qk_cols, z_cols, xbc_cols, nb):
    tm = h_ref.shape[0]
    hn = _rms(h_ref[...], g_ref[...]).astype(BF16)
    hn_ext = jnp.concatenate([_rms(hp_ref[...], g_ref[...]).astype(BF16), hn,
                              _rms(hx_ref[...], g_ref[...]).astype(BF16)], axis=0)
    i = pl.program_id(0) % nb
    row = lax.broadcasted_iota(jnp.int32, (tm + 2 * SUBLANES, 1), 0)
    inside = jnp.logical_and(jnp.logical_or(i > 0, row >= SUBLANES),
                             jnp.logical_or(i < nb - 1, row < tm + SUBLANES))
    pad = SSD_CONV // 2
    c_z = 2 * qk_cols
    c_x = c_z + z_cols
    c_dt = c_x + xbc_cols
    cos = cos_ref[...]
    sa = sa_ref[...]
    sb = sb_ref[...]

    def conv_dot(c):
        ext = _dot(hn_ext, w_ref[:, c_x + c * MXU_COLS:c_x + (c + 1) * MXU_COLS])
        return jnp.where(inside, ext, 0.0)

    def conv_taps(c, ext):
        cols = slice(c * MXU_COLS, (c + 1) * MXU_COLS)
        for r0 in range(0, tm, CONV_ROWS):
            piece = ext[r0:r0 + CONV_ROWS + 2 * SUBLANES]
            acc = jnp.zeros((CONV_ROWS, MXU_COLS), F32) + cb_ref[:, cols]
            for k in range(SSD_CONV):
                off = SUBLANES - pad + k
                acc = acc + piece[off:off + CONV_ROWS] * cw_ref[k:k + 1, cols]
            xc_ref[r0:r0 + CONV_ROWS, cols] = _silu(acc).astype(xc_ref.dtype)

    def rotary_block(col0, out_ref, mult, c):
        t2 = _dot(hn, w_ref[:, col0 + c * MXU_COLS:col0 + (c + 1) * MXU_COLS])
        for half in range(MXU_COLS // LANES):
            t = t2[:, half * LANES:(half + 1) * LANES]
            r = t * cos + pltpu.roll(t, LANES - ROT_DIM // 2, 1) * sa \
                + pltpu.roll(t, ROT_DIM // 2, 1) * sb
            out_ref[0, c * (MXU_COLS // LANES) + half] = (r * mult).astype(out_ref.dtype)

    def z_block(c):
        cols = slice(c * MXU_COLS, (c + 1) * MXU_COLS)
        z_ref[:, cols] = _dot(hn, w_ref[:, c_z + c * MXU_COLS:c_z + (c + 1) * MXU_COLS]
                              ).astype(z_ref.dtype)

    n_conv = xbc_cols // MXU_COLS
    n_rot = qk_cols // MXU_COLS
    units = ([functools.partial(rotary_block, 0, q_ref, ATTN_QK_DIM ** -0.5 * LOG2_E, c)
              for c in range(n_rot)]
             + [functools.partial(rotary_block, qk_cols, k_ref, 1.0, c) for c in range(n_rot)]
             + [functools.partial(z_block, c) for c in range(z_cols // MXU_COLS)])
    per_conv = -(-len(units) // n_conv)
    for c in range(n_conv):
        ext = conv_dot(c)
        for unit in units[c * per_conv:(c + 1) * per_conv]:
            unit()
        conv_taps(c, ext)
    vt_ref[0] = _dot_nt(wvt_ref[...], hn).astype(vt_ref.dtype).reshape(vt_ref.shape[1:])
    dt_raw = _dot(hn, w_ref[:, c_dt:c_dt + LANES]) + dtb_ref[...]
    dt_ref[...] = jax.nn.softplus(dt_raw)


def _in_proj(h2d, g, w_pad, wvt, cos, sa, sb, dtb, conv_w, conv_b, tm, seq, qk_cols, z_cols,
             xbc_cols):
    t, d = h2d.shape
    per = tm // SUBLANES
    n_halo = t // SUBLANES
    prev = pl.BlockSpec((SUBLANES, d), lambda i: (jnp.maximum(i * per - 1, 0), 0))
    nxt = pl.BlockSpec((SUBLANES, d), lambda i: (jnp.minimum((i + 1) * per, n_halo - 1), 0))
    v_cols = wvt.shape[0]
    nb = seq // tm
    row = lambda n: pl.BlockSpec((tm, n), lambda i: (i, 0))
    pos = pl.BlockSpec((tm, LANES), lambda i: (i % nb, 0))
    heads = qk_cols // LANES
    qk_spec = pl.BlockSpec((1, heads, tm, LANES), lambda i: (i // nb, 0, i % nb, 0))
    vt_spec = pl.BlockSpec((1, heads, v_cols // heads, tm), lambda i: (i // nb, 0, 0, i % nb))
    kern = functools.partial(_in_proj_kernel, qk_cols=qk_cols, z_cols=z_cols,
                             xbc_cols=xbc_cols, nb=nb)
    return pl.pallas_call(
        kern,
        grid=(t // tm,),
        in_specs=[prev, row(d), nxt, _resident((1, d)), _resident(w_pad.shape),
                  _resident(wvt.shape), pos, pos, pos, _resident((1, LANES)),
                  _resident(conv_w.shape), _resident(conv_b.shape)],
        out_specs=[qk_spec, qk_spec, vt_spec, row(z_cols), row(xbc_cols), row(LANES)],
        out_shape=[jax.ShapeDtypeStruct((t // seq, heads, seq, LANES), BF16),
                   jax.ShapeDtypeStruct((t // seq, heads, seq, LANES), BF16),
                   jax.ShapeDtypeStruct((t // seq, heads, v_cols // heads, seq), BF16),
                   jax.ShapeDtypeStruct((t, z_cols), BF16),
                   jax.ShapeDtypeStruct((t, xbc_cols), BF16),
                   jax.ShapeDtypeStruct((t, LANES), F32)],
        compiler_params=_params("parallel"),
        name="in_proj",
    )(h2d, h2d, h2d, g, w_pad, wvt, cos, sa, sb, dtb, conv_w, conv_b)


def _attn_kernel(q_ref, k_ref, vt_ref, lam_ref, g_ref, o_ref, st_a_ref, st_b_ref, *, lam_init):
    heads, seq = k_ref.shape[1], k_ref.shape[2]
    n_q = seq // ATTN_QB
    n_kb = seq // ATTN_KB
    n_steps = heads * n_q
    lp = lam_ref[...]
    lam = (jnp.exp(jnp.sum(lp[0:1] * lp[1:2], axis=-1, keepdims=True))
           - jnp.exp(jnp.sum(lp[2:3] * lp[3:4], axis=-1, keepdims=True)) + lam_init)
    first = lax.broadcasted_iota(jnp.int32, (ATTN_QB, ATTN_V_DIM), 1) < ATTN_QK_DIM
    ones_rows = jnp.ones((BF16_ROWS, ATTN_KB), BF16)

    def query_rows(c):
        return pl.ds(pl.multiple_of((c % n_q) * ATTN_QB, ATTN_QB), ATTN_QB)

    def both_maps(c):
        q = q_ref[0, c // n_q, query_rows(c), :]
        zero = jnp.zeros_like(q)
        return jnp.concatenate([jnp.where(first, q, zero), jnp.where(first, zero, q)], axis=0)

    def scores_block(c, q2, st_ref, kb, m8):
        keys = slice(kb * ATTN_KB, (kb + 1) * ATTN_KB)
        st = _dot_nt(k_ref[0, c // n_q, keys, :], q2)
        st_ref[keys, :] = st
        return jnp.maximum(m8, jnp.max(st.reshape(ATTN_KB // 8, 8, 2 * ATTN_QB), axis=0))

    def values_block(c, st_ref, kb, m, acc):
        keys = slice(kb * ATTN_KB, (kb + 1) * ATTN_KB)
        p = jnp.exp2(st_ref[keys, :] - m).astype(BF16)
        vt_aug = jnp.concatenate([vt_ref[0, c // n_q, :, keys], ones_rows], axis=0)
        return acc + _dot(vt_aug, p)

    def finish(c, acc):
        o1 = acc[:ATTN_V_DIM, :ATTN_QB] * (1.0 / acc[ATTN_V_DIM:ATTN_V_DIM + 1, :ATTN_QB])
        o2 = acc[:ATTN_V_DIM, ATTN_QB:] * (1.0 / acc[ATTN_V_DIM:ATTN_V_DIM + 1, ATTN_QB:])
        o = (o1 - lam * o2).T
        o_ref[0, c // n_q, query_rows(c), :] = (
            _rms(o, g_ref[...]) * (1.0 - lam_init)).astype(o_ref.dtype)

    m8_init = jnp.full((8, 2 * ATTN_QB), -jnp.inf, F32)
    acc_init = jnp.zeros((ATTN_V_DIM + BF16_ROWS, 2 * ATTN_QB), F32)
    buffers = (st_a_ref, st_b_ref)

    def scores_only(c):
        q2 = both_maps(c)
        m8 = m8_init
        for kb in range(n_kb):
            m8 = scores_block(c, q2, buffers[0], kb, m8)
        return jnp.max(m8, axis=0, keepdims=True)

    def values_only(c, parity, m):
        acc = acc_init
        for kb in range(n_kb):
            acc = values_block(c, buffers[parity], kb, m, acc)
        finish(c, acc)

    def overlapped(c, parity, m_prev):
        q2 = both_maps(c)
        m8, acc = m8_init, acc_init
        for kb in range(n_kb):
            m8 = scores_block(c, q2, buffers[parity], kb, m8)
            acc = values_block(c - 1, buffers[1 - parity], kb, m_prev, acc)
        finish(c - 1, acc)
        return jnp.max(m8, axis=0, keepdims=True)

    m = scores_only(0)
    n_trips = (n_steps - 1) // ATTN_STEPS_PER_TRIP

    def trip(i, m):
        for j in range(ATTN_STEPS_PER_TRIP):
            m = overlapped(i * ATTN_STEPS_PER_TRIP + 1 + j, (1 + j) % 2, m)
        return m

    m = lax.fori_loop(0, n_trips, trip, m)
    for c in range(n_trips * ATTN_STEPS_PER_TRIP + 1, n_steps):
        m = overlapped(c, c % 2, m)
    values_only(n_steps - 1, (n_steps - 1) % 2, m)


def _attention(q, k, vt, lam_params, g, lam_init):
    bsz, heads, seq, _ = q.shape
    qo = pl.BlockSpec((1, heads, seq, ATTN_V_DIM), lambda b: (b, 0, 0, 0))
    vt_spec = pl.BlockSpec((1, heads, ATTN_V_DIM, seq), lambda b: (b, 0, 0, 0))
    kern = functools.partial(_attn_kernel, lam_init=lam_init)
    return pl.pallas_call(
        kern,
        grid=(bsz,),
        in_specs=[qo, qo, vt_spec, _resident(lam_params.shape), _resident(g.shape)],
        out_specs=qo,
        out_shape=jax.ShapeDtypeStruct(q.shape, BF16),
        scratch_shapes=[pltpu.VMEM((seq, 2 * ATTN_QB), F32), pltpu.VMEM((seq, 2 * ATTN_QB), F32)],
        compiler_params=_params("parallel"),
        name="attention",
    )(q, k, vt, lam_params, g)


def _cumsum_rows(x, reverse):
    n = x.shape[0]
    row = lax.broadcasted_iota(jnp.int32, x.shape, 0)
    step = 1
    while step < n:
        if reverse:
            shifted = pltpu.roll(x, n - step, 0)
            x = x + jnp.where(row < n - step, shifted, 0.0)
        else:
            shifted = pltpu.roll(x, step, 0)
            x = x + jnp.where(row >= step, shifted, 0.0)
        step *= 2
    return x


def _ssd_direction(xc_ref, dt_ref, alog_ref, state_ref, y_ref, tiles_ref, cb_ref, bt_ref, *,
                   lane0, reverse):
    L = SSD_CHUNK
    n_x = SSD_HEADS * SSD_HEAD_DIM
    gw = SSD_STATE
    dt = dt_ref[0]
    a = -jnp.exp(alog_ref[...]) * LOG2_E
    cs = _cumsum_rows(dt * a, reverse)
    tot = cs[0:1] if reverse else cs[L - 1:L]
    w = dt * jnp.exp2(tot - cs)
    log_dt = jnp.log2(dt)
    tiles_ref[0] = cs
    tiles_ref[1] = (cs - log_dt).T
    tiles_ref[2] = log_dt.T
    tiles_ref[3] = w.T
    row = lax.broadcasted_iota(jnp.int32, (L, L), 0)
    col = lax.broadcasted_iota(jnp.int32, (L, L), 1)
    keep = (row <= col) if reverse else (row >= col)
    first = lax.broadcasted_iota(jnp.int32, (L, LANES), 1) < SSD_HEAD_DIM

    def group_cols(kind, g):
        lo = n_x + (kind * SSD_GROUPS + g) * gw
        return slice(lo, lo + gw)

    for g in range(SSD_GROUPS):
        b_g = xc_ref[0, :, group_cols(0, g)]
        cb_ref[g] = jnp.where(keep, _dot_nt(xc_ref[0, :, group_cols(1, g)], b_g), 0.0
                              ).astype(BF16)
        bt_ref[g] = b_g.astype(F32).T.astype(BF16)

    heads_per_group = SSD_HEADS // SSD_GROUPS
    for pair in range(SSD_HEADS // 2):
        cols = slice(pair * LANES, (pair + 1) * LANES)
        xs = xc_ref[0, :, cols]
        prev = state_ref[:, cols]
        rhs = jnp.concatenate([xs, prev.astype(BF16)], axis=0)
        ys, ss, cds = [], [], []
        for h in (2 * pair, 2 * pair + 1):
            g = h // heads_per_group
            hl = lane0 + h
            col_cs = jnp.broadcast_to(tiles_ref[0, :, hl:hl + 1], (L, L))
            row_cs = jnp.broadcast_to(tiles_ref[1, hl:hl + 1, :], (L, L))
            row_ldt = jnp.broadcast_to(tiles_ref[2, hl:hl + 1, :], (L, L))
            decay_dt = jnp.exp2(jnp.minimum(col_cs - row_cs, row_ldt))
            m = cb_ref[g] * decay_dt.astype(BF16)
            c_e = xc_ref[0, :, group_cols(1, g)] * jnp.exp2(col_cs).astype(BF16)
            ys.append(_dot(jnp.concatenate([m, c_e], axis=1), rhs))
            b_w = bt_ref[g] * jnp.broadcast_to(tiles_ref[3, hl:hl + 1, :], (gw, L)).astype(BF16)
            ss.append(_dot(b_w, xs))
            cds.append(jnp.broadcast_to(jnp.exp2(tot[:, hl:hl + 1]), (1, LANES)))
        y_ref[0, :, cols] = jnp.where(first, ys[0], ys[1]).astype(y_ref.dtype)
        chunk_decay = jnp.where(first[0:1], cds[0], cds[1])
        state_ref[:, cols] = prev * chunk_decay + jnp.where(first, ss[0], ss[1])


def _ssd_kernel(xf_ref, xb_ref, dtf_ref, dtb_ref, alog_ref, yf_ref, yb_ref, sf_ref, sb_ref,
                tiles_f_ref, tiles_b_ref, cb_f_ref, cb_b_ref, bt_f_ref, bt_b_ref):
    @pl.when(pl.program_id(1) == 0)
    def _():
        sf_ref[...] = jnp.zeros_like(sf_ref)
        sb_ref[...] = jnp.zeros_like(sb_ref)

    _ssd_direction(xf_ref, dtf_ref, alog_ref, sf_ref, yf_ref, tiles_f_ref, cb_f_ref, bt_f_ref,
                   lane0=0, reverse=False)
    _ssd_direction(xb_ref, dtb_ref, alog_ref, sb_ref, yb_ref, tiles_b_ref, cb_b_ref, bt_b_ref,
                   lane0=SSD_HEADS, reverse=True)


def _ssd(xconv, dt, alog):
    bsz, seq, n_ch = xconv.shape
    n_x = SSD_HEADS * SSD_HEAD_DIM
    nc = seq // SSD_CHUNK
    fwd = lambda n: pl.BlockSpec((1, SSD_CHUNK, n), lambda b, t: (b, t, 0))
    bwd = lambda n: pl.BlockSpec((1, SSD_CHUNK, n), lambda b, t: (b, nc - 1 - t, 0))
    y_shape = jax.ShapeDtypeStruct((bsz, seq, n_x), BF16)
    return pl.pallas_call(
        _ssd_kernel,
        grid=(bsz, nc),
        in_specs=[fwd(n_ch), bwd(n_ch), fwd(LANES), bwd(LANES), _resident(alog.shape)],
        out_specs=[fwd(n_x), bwd(n_x)],
        out_shape=[y_shape, y_shape],
        scratch_shapes=[pltpu.VMEM((SSD_STATE, n_x), F32), pltpu.VMEM((SSD_STATE, n_x), F32),
                        pltpu.VMEM((4, SSD_CHUNK, LANES), F32), pltpu.VMEM((4, SSD_CHUNK, LANES), F32),
                        pltpu.VMEM((SSD_GROUPS, SSD_CHUNK, SSD_CHUNK), BF16),
                        pltpu.VMEM((SSD_GROUPS, SSD_CHUNK, SSD_CHUNK), BF16),
                        pltpu.VMEM((SSD_GROUPS, SSD_STATE, SSD_CHUNK), BF16),
                        pltpu.VMEM((SSD_GROUPS, SSD_STATE, SSD_CHUNK), BF16)],
        compiler_params=_params("parallel", "arbitrary"),
        name="ssd",
    )(xconv, xconv, dt, dt, alog)


def _out_ffn_kernel(h_ref, attn_ref, yf_ref, yb_ref, xs_ref, z_ref, dskip_ref, gssd_ref,
                    wo_ref, gmix_ref, gpre_ref, wg_ref, wu_ref, wd_ref, gpost_ref,
                    gfinal_ref, o_ref):
    heads = attn_ref.shape[1]
    attn = jnp.concatenate([attn_ref[0, h] for h in range(heads)], axis=1)
    n_attn = attn.shape[-1]
    y = (yf_ref[...].astype(F32) + yb_ref[...].astype(F32)
         + xs_ref[...].astype(F32) * dskip_ref[...])
    y = _rms(y * _silu(z_ref[...].astype(F32)), gssd_ref[...])
    m = _dot(attn, wo_ref[:n_attn, :]) + _dot(y.astype(BF16), wo_ref[n_attn:, :])
    h = h_ref[...] + _rms(m, gmix_ref[...])
    h = _swiglu_half_step(h, gpre_ref[...], wg_ref, wu_ref, wd_ref, gpost_ref[...])
    o_ref[...] = _rms(h, gfinal_ref[...])


def _out_ffn(h2d, attn, yf, yb, xconv, z, dskip, gssd, wo, gmix, gpre, wg, wu, wd, gpost,
             gfinal, tm):
    t, d = h2d.shape
    _, heads, seq, v_dim = attn.shape
    nb = seq // tm
    attn_spec = pl.BlockSpec((1, heads, tm, v_dim), lambda i: (i // nb, 0, i % nb, 0))
    f = wg.shape[1]
    n_x = yf.shape[-1]
    row = lambda n: pl.BlockSpec((tm, n), lambda i: (i, 0))
    return pl.pallas_call(
        _out_ffn_kernel,
        grid=(t // tm,),
        in_specs=[row(d), attn_spec, row(n_x), row(n_x), row(n_x), row(n_x),
                  _resident((1, n_x)), _resident((1, n_x)), _resident(wo.shape),
                  _resident((1, d)), _resident((1, d)), _resident((d, f)),
                  _resident((d, f)), _resident((f, d)), _resident((1, d)),
                  _resident((1, d))],
        out_specs=row(d),
        out_shape=jax.ShapeDtypeStruct((t, d), F32),
        compiler_params=_params("parallel"),
        name="out_ffn",
    )(h2d, attn, yf, yb, xconv, z, dskip, gssd, wo, gmix, gpre, wg, wu, wd, gpost, gfinal)


def _rotary_tables(seq):
    half = ROT_DIM // 2
    pos = jnp.arange(seq, dtype=F32)
    inv_freq = jnp.power(F32(ROPE_THETA), -jnp.arange(0, ROT_DIM, 2, dtype=F32) / ROT_DIM)
    ang = pos[:, None] * inv_freq[None, :]
    cos, sin = jnp.cos(ang), jnp.sin(ang)
    ones = jnp.ones((seq, ATTN_QK_DIM - ROT_DIM), F32)
    zeros_h = jnp.zeros((seq, half), F32)
    zeros_r = jnp.zeros((seq, ATTN_QK_DIM - ROT_DIM), F32)
    reps = LANES // ATTN_QK_DIM
    cos_t = jnp.tile(jnp.concatenate([cos, cos, ones], axis=1), (1, reps))
    sa_t = jnp.tile(jnp.concatenate([-sin, zeros_h, zeros_r], axis=1), (1, reps))
    sb_t = jnp.tile(jnp.concatenate([zeros_h, sin, zeros_r], axis=1), (1, reps))
    return cos_t, sa_t, sb_t


def _pad_lanes(v, width=LANES):
    return jnp.pad(v, ((0, 0), (0, width - v.shape[-1])))


def _layer(h2d, layer_idx, p, bsz, seq):
    t, d = h2d.shape
    tm = min(ROW_TILE, seq)
    row = lambda v: v.reshape(1, -1).astype(F32)
    bf = lambda w: w.astype(BF16)

    h1 = _ffn(h2d, row(p["ffn1_pre_g"]), bf(p["ffn1_w_gate"]), bf(p["ffn1_w_up"]),
              bf(p["ffn1_w_down"]), row(p["ffn1_post_g"]), tm)

    qk_cols = ATTN_HEADS * 2 * ATTN_QK_DIM
    v_cols = ATTN_HEADS * ATTN_V_DIM
    n_x = SSD_HEADS * SSD_HEAD_DIM
    xbc_cols = n_x + 2 * SSD_GROUPS * SSD_STATE
    w_in = p["w_in"]
    v0 = 2 * qk_cols
    wvt = bf(w_in[:, v0:v0 + v_cols].T)
    n_dt = w_in.shape[1] - (v0 + v_cols + n_x + xbc_cols)
    w_pad = bf(jnp.pad(jnp.concatenate([w_in[:, :v0], w_in[:, v0 + v_cols:]], axis=1),
                       ((0, 0), (0, LANES - n_dt))))
    dt_bias = _pad_lanes(jnp.concatenate([p["dt_bias_fwd"], p["dt_bias_bwd"]]).reshape(1, -1)
                         .astype(F32))
    alog = _pad_lanes(jnp.concatenate([p["a_log_fwd"], p["a_log_bwd"]]).reshape(1, -1)
                      .astype(F32))
    cos_t, sa_t, sb_t = _rotary_tables(seq)
    q, k, vt, z, xconv, dt = _in_proj(h1, row(p["mix_pre_g"]), w_pad, wvt, cos_t, sa_t, sb_t,
                                      dt_bias, p["conv_w"].astype(F32), row(p["conv_b"]), tm, seq,
                                      qk_cols, n_x, xbc_cols)
    shape3 = lambda a: a.reshape(bsz, seq, a.shape[-1])
    xconv = shape3(xconv)

    lam_init = 0.8 - 0.6 * math.exp(-0.3 * layer_idx)
    lam_params = jnp.stack([p["lambda_q1"], p["lambda_k1"], p["lambda_q2"],
                            p["lambda_k2"]]).astype(F32)
    attn = _attention(q, k, vt, lam_params, row(p["attn_subln_g"]), lam_init)

    yf, yb = _ssd(xconv, shape3(dt), alog)

    dskip = jnp.repeat(p["d_skip"].astype(F32), SSD_HEAD_DIM).reshape(1, -1)
    flat = lambda a: a.reshape(t, a.shape[-1])
    return _out_ffn(h1, attn, flat(yf), flat(yb), flat(xconv), z, dskip,
                    row(p["ssd_norm_g"]), bf(p["w_out"]), row(p["mix_post_g"]),
                    row(p["ffn2_pre_g"]), bf(p["ffn2_w_gate"]), bf(p["ffn2_w_up"]),
                    bf(p["ffn2_w_down"]), row(p["ffn2_post_g"]), row(p["final_g"]), tm)


def kernel(x, ffn1_pre_g, ffn1_w_gate, ffn1_w_up, ffn1_w_down, ffn1_post_g, mix_pre_g, w_in, lambda_q1, lambda_k1, lambda_q2, lambda_k2, attn_subln_g, conv_w, conv_b, a_log_fwd, a_log_bwd, dt_bias_fwd, dt_bias_bwd, d_skip, ssd_norm_g, w_out, mix_post_g, ffn2_pre_g, ffn2_w_gate, ffn2_w_up, ffn2_w_down, ffn2_post_g, final_g):
    names = ("ffn1_pre_g", "ffn1_w_gate", "ffn1_w_up", "ffn1_w_down", "ffn1_post_g",
             "mix_pre_g", "w_in", "lambda_q1", "lambda_k1", "lambda_q2", "lambda_k2",
             "attn_subln_g", "conv_w", "conv_b", "a_log_fwd", "a_log_bwd", "dt_bias_fwd",
             "dt_bias_bwd", "d_skip", "ssd_norm_g", "w_out", "mix_post_g", "ffn2_pre_g",
             "ffn2_w_gate", "ffn2_w_up", "ffn2_w_down", "ffn2_post_g", "final_g")
    stacked = dict(zip(names, (ffn1_pre_g, ffn1_w_gate, ffn1_w_up, ffn1_w_down, ffn1_post_g,
                               mix_pre_g, w_in, lambda_q1, lambda_k1, lambda_q2, lambda_k2,
                               attn_subln_g, conv_w, conv_b, a_log_fwd, a_log_bwd,
                               dt_bias_fwd, dt_bias_bwd, d_skip, ssd_norm_g, w_out,
                               mix_post_g, ffn2_pre_g, ffn2_w_gate, ffn2_w_up, ffn2_w_down,
                               ffn2_post_g, final_g)))
    bsz, seq, d = x.shape
    h = x.reshape(bsz * seq, d)
    for i in range(ffn1_pre_g.shape[0]):
        h = _layer(h, i, {n: a[i] for n, a in stacked.items()}, bsz, seq)
    return h.reshape(bsz, seq, d)
```

```python
import functools
import math

import jax
import jax.numpy as jnp
from jax import lax
from jax.experimental import pallas as pl
from jax.experimental.pallas import tpu as pltpu

F32 = jnp.float32
BF16 = jnp.bfloat16

EPS = 1e-6
ATTN_HEADS = 8
ATTN_QK_DIM = 64
ATTN_V_DIM = 2 * ATTN_QK_DIM
ROT_DIM = ATTN_QK_DIM // 4
ROPE_THETA = 500000.0
SSD_HEADS = 16
SSD_HEAD_DIM = 64
SSD_GROUPS = 2
SSD_STATE = 128
SSD_CONV = 5
SSD_CHUNK = 128
SSD_CHUNKS_PER_STEP = 4

LANES = 128
MXU_COLS = 256
BF16_ROWS = 16
LOG2_E = math.log2(math.e)
ATTN_QB = MXU_COLS
ATTN_KB = MXU_COLS
ATTN_STEPS_PER_TRIP = 8
ROW_TILE = 512
FFN_ROW_GROUPS = 2
CONV_ROWS = 64
SUBLANES = 8
VMEM_LIMIT_BYTES = 56 * 1024 * 1024


def _params(*semantics):
    return pltpu.CompilerParams(dimension_semantics=semantics,
                                vmem_limit_bytes=VMEM_LIMIT_BYTES)


def _resident(shape):
    zeros = (0,) * len(shape)
    return pl.BlockSpec(shape, lambda *_: zeros, pipeline_mode=pl.Buffered(1))


def _rms(x, g):
    return x * lax.rsqrt(jnp.mean(x * x, axis=-1, keepdims=True) + EPS) * g


def _silu(x):
    return x * jax.nn.sigmoid(x)


def _dot(a, b):
    return jnp.dot(a, b, preferred_element_type=F32)


def _dot_nt(a, b):
    return lax.dot_general(a, b, (((1,), (1,)), ((), ())), preferred_element_type=F32)


def _swiglu_half_step(x, gpre, wg_ref, wu_ref, wd_ref, gpost):
    outs = []
    rows = x.shape[0] // FFN_ROW_GROUPS
    xn = _rms(x, gpre).astype(BF16)
    acts = []
    for r in range(FFN_ROW_GROUPS):
        xr = xn[r * rows:(r + 1) * rows]
        acts.append((_silu(_dot(xr, wg_ref[...])) * _dot(xr, wu_ref[...])).astype(BF16))
    for r in range(FFN_ROW_GROUPS):
        f = _dot(acts[r], wd_ref[...])
        outs.append(x[r * rows:(r + 1) * rows] + 0.5 * _rms(f, gpost))
    return jnp.concatenate(outs, axis=0)


def _ffn_kernel(x_ref, gpre_ref, wg_ref, wu_ref, wd_ref, gpost_ref, o_ref):
    o_ref[...] = _swiglu_half_step(x_ref[...], gpre_ref[...], wg_ref, wu_ref, wd_ref,
                                   gpost_ref[...])


def _ffn(x2d, gpre, wg, wu, wd, gpost, tm):
    t, d = x2d.shape
    f = wg.shape[1]
    row = pl.BlockSpec((tm, d), lambda i: (i, 0))
    return pl.pallas_call(
        _ffn_kernel,
        grid=(t // tm,),
        in_specs=[row, _resident((1, d)), _resident((d, f)), _resident((d, f)),
                  _resident((f, d)), _resident((1, d))],
        out_specs=row,
        out_shape=jax.ShapeDtypeStruct((t, d), F32),
        compiler_params=_params("parallel"),
        name="ffn1",
    )(x2d, gpre, wg, wu, wd, gpost)


def _in_proj_kernel(hp_ref, h_ref, hx_ref, g_ref, w_ref, wvt_ref, cos_ref, sa_ref, sb_ref,
                    dtb_ref, cw_ref, cb_ref, q_ref, k_ref, vt_ref, z_ref, xc_ref, dt_ref, *,
                    qk_cols, z_cols, xbc_cols, nb):
    tm = h_ref.shape[0]
    hn = _rms(h_ref[...], g_ref[...]).astype(BF16)
    hn_ext = jnp.concatenate([_rms(hp_ref[...], g_ref[...]).astype(BF16), hn,
                              _rms(hx_ref[...], g_ref[...]).astype(BF16)], axis=0)
    i = pl.program_id(0) % nb
    row = lax.broadcasted_iota(jnp.int32, (tm + 2 * SUBLANES, 1), 0)
    inside = jnp.logical_and(jnp.logical_or(i > 0, row >= SUBLANES),
                             jnp.logical_or(i < nb - 1, row < tm + SUBLANES))
    pad = SSD_CONV // 2
    c_z = 2 * qk_cols
    c_x = c_z + z_cols
    c_dt = c_x + xbc_cols
    cos = cos_ref[...]
    sa = sa_ref[...]
    sb = sb_ref[...]

    def conv_dot(c):
        ext = _dot(hn_ext, w_ref[:, c_x + c * MXU_COLS:c_x + (c + 1) * MXU_COLS])
        return jnp.where(inside, ext, 0.0)

    def conv_taps(c, ext):
        cols = slice(c * MXU_COLS, (c + 1) * MXU_COLS)
        for r0 in range(0, tm, CONV_ROWS):
            piece = ext[r0:r0 + CONV_ROWS + 2 * SUBLANES]
            acc = jnp.zeros((CONV_ROWS, MXU_COLS), F32) + cb_ref[:, cols]
            for k in range(SSD_CONV):
                off = SUBLANES - pad + k
                acc = acc + piece[off:off + CONV_ROWS] * cw_ref[k:k + 1, cols]
            xc_ref[r0:r0 + CONV_ROWS, cols] = _silu(acc).astype(xc_ref.dtype)

    def rotary_block(col0, out_ref, mult, c):
        t2 = _dot(hn, w_ref[:, col0 + c * MXU_COLS:col0 + (c + 1) * MXU_COLS])
        for half in range(MXU_COLS // LANES):
            t = t2[:, half * LANES:(half + 1) * LANES]
            r = t * cos + pltpu.roll(t, LANES - ROT_DIM // 2, 1) * sa \
                + pltpu.roll(t, ROT_DIM // 2, 1) * sb
            out_ref[0, c * (MXU_COLS // LANES) + half] = (r * mult).astype(out_ref.dtype)

    def z_block(c):
        cols = slice(c * MXU_COLS, (c + 1) * MXU_COLS)
        z_ref[:, cols] = _dot(hn, w_ref[:, c_z + c * MXU_COLS:c_z + (c + 1) * MXU_COLS]
                              ).astype(z_ref.dtype)

    n_conv = xbc_cols // MXU_COLS
    n_rot = qk_cols // MXU_COLS
    units = ([functools.partial(rotary_block, 0, q_ref, ATTN_QK_DIM ** -0.5 * LOG2_E, c)
              for c in range(n_rot)]
             + [functools.partial(rotary_block, qk_cols, k_ref, 1.0, c) for c in range(n_rot)]
             + [functools.partial(z_block, c) for c in range(z_cols // MXU_COLS)])
    per_conv = -(-len(units) // n_conv)
    for c in range(n_conv):
        ext = conv_dot(c)
        for unit in units[c * per_conv:(c + 1) * per_conv]:
            unit()
        conv_taps(c, ext)
    vt_ref[0] = _dot_nt(wvt_ref[...], hn).astype(vt_ref.dtype).reshape(vt_ref.shape[1:])
    dt_raw = _dot(hn, w_ref[:, c_dt:c_dt + LANES]) + dtb_ref[...]
    dt_ref[...] = jax.nn.softplus(dt_raw)


def _in_proj(h2d, g, w_pad, wvt, cos, sa, sb, dtb, conv_w, conv_b, tm, seq, qk_cols, z_cols,
             xbc_cols):
    t, d = h2d.shape
    per = tm // SUBLANES
    n_halo = t // SUBLANES
    prev = pl.BlockSpec((SUBLANES, d), lambda i: (jnp.maximum(i * per - 1, 0), 0))
    nxt = pl.BlockSpec((SUBLANES, d), lambda i: (jnp.minimum((i + 1) * per, n_halo - 1), 0))
    v_cols = wvt.shape[0]
    nb = seq // tm
    row = lambda n: pl.BlockSpec((tm, n), lambda i: (i, 0))
    pos = pl.BlockSpec((tm, LANES), lambda i: (i % nb, 0))
    heads = qk_cols // LANES
    qk_spec = pl.BlockSpec((1, heads, tm, LANES), lambda i: (i // nb, 0, i % nb, 0))
    vt_spec = pl.BlockSpec((1, heads, v_cols // heads, tm), lambda i: (i // nb, 0, 0, i % nb))
    kern = functools.partial(_in_proj_kernel, qk_cols=qk_cols, z_cols=z_cols,
                             xbc_cols=xbc_cols, nb=nb)
    return pl.pallas_call(
        kern,
        grid=(t // tm,),
        in_specs=[prev, row(d), nxt, _resident((1, d)), _resident(w_pad.shape),
                  _resident(wvt.shape), pos, pos, pos, _resident((1, LANES)),
                  _resident(conv_w.shape), _resident(conv_b.shape)],
        out_specs=[qk_spec, qk_spec, vt_spec, row(z_cols), row(xbc_cols), row(LANES)],
        out_shape=[jax.ShapeDtypeStruct((t // seq, heads, seq, LANES), BF16),
                   jax.ShapeDtypeStruct((t // seq, heads, seq, LANES), BF16),
                   jax.ShapeDtypeStruct((t // seq, heads, v_cols // heads, seq), BF16),
                   jax.ShapeDtypeStruct((t, z_cols), BF16),
                   jax.ShapeDtypeStruct((t, xbc_cols), BF16),
                   jax.ShapeDtypeStruct((t, LANES), F32)],
        compiler_params=_params("parallel"),
        name="in_proj",
    )(h2d, h2d, h2d, g, w_pad, wvt, cos, sa, sb, dtb, conv_w, conv_b)


def _attn_kernel(q_ref, k_ref, vt_ref, lam_ref, g_ref, o_ref, st_a_ref, st_b_ref, *, lam_init):
    heads, seq = k_ref.shape[1], k_ref.shape[2]
    n_q = seq // ATTN_QB
    n_kb = seq // ATTN_KB
    n_steps = heads * n_q
    lp = lam_ref[...]
    lam = (jnp.exp(jnp.sum(lp[0:1] * lp[1:2], axis=-1, keepdims=True))
           - jnp.exp(jnp.sum(lp[2:3] * lp[3:4], axis=-1, keepdims=True)) + lam_init)
    first = lax.broadcasted_iota(jnp.int32, (ATTN_V_DIM, ATTN_QB), 0) < ATTN_QK_DIM
    ones_rows = jnp.ones((BF16_ROWS, ATTN_KB), BF16)

    def query_rows(c):
        return pl.ds(pl.multiple_of((c % n_q) * ATTN_QB, ATTN_QB), ATTN_QB)

    def both_maps(c):
        qt = q_ref[0, c // n_q, query_rows(c), :].astype(F32).T.astype(BF16)
        zero = jnp.zeros_like(qt)
        return jnp.concatenate([jnp.where(first, qt, zero), jnp.where(first, zero, qt)], axis=1)

    def scores_block(c, q2, st_ref, kb, m8):
        keys = slice(kb * ATTN_KB, (kb + 1) * ATTN_KB)
        st = _dot(k_ref[0, c // n_q, keys, :], q2)
        st_ref[keys, :] = st
        return jnp.maximum(m8, jnp.max(st.reshape(ATTN_KB // 8, 8, 2 * ATTN_QB), axis=0))

    def values_block(c, st_ref, kb, m, acc):
        keys = slice(kb * ATTN_KB, (kb + 1) * ATTN_KB)
        p = jnp.exp2(st_ref[keys, :] - m).astype(BF16)
        vt_aug = jnp.concatenate([vt_ref[0, c // n_q, :, keys], ones_rows], axis=0)
        return acc + _dot(vt_aug, p)

    def finish(c, acc):
        o1 = acc[:ATTN_V_DIM, :ATTN_QB] * (1.0 / acc[ATTN_V_DIM:ATTN_V_DIM + 1, :ATTN_QB])
        o2 = acc[:ATTN_V_DIM, ATTN_QB:] * (1.0 / acc[ATTN_V_DIM:ATTN_V_DIM + 1, ATTN_QB:])
        o = (o1 - lam * o2).T
        o_ref[0, c // n_q, query_rows(c), :] = (
            _rms(o, g_ref[...]) * (1.0 - lam_init)).astype(o_ref.dtype)

    m8_init = jnp.full((8, 2 * ATTN_QB), -jnp.inf, F32)
    acc_init = jnp.zeros((ATTN_V_DIM + BF16_ROWS, 2 * ATTN_QB), F32)
    buffers = (st_a_ref, st_b_ref)

    def scores_only(c):
        q2 = both_maps(c)
        m8 = m8_init
        for kb in range(n_kb):
            m8 = scores_block(c, q2, buffers[0], kb, m8)
        return jnp.max(m8, axis=0, keepdims=True)

    def values_only(c, parity, m):
        acc = acc_init
        for kb in range(n_kb):
            acc = values_block(c, buffers[parity], kb, m, acc)
        finish(c, acc)

    def overlapped(c, parity, m_prev):
        q2 = both_maps(c)
        m8, acc = m8_init, acc_init
        for kb in range(n_kb):
            m8 = scores_block(c, q2, buffers[parity], kb, m8)
            acc = values_block(c - 1, buffers[1 - parity], kb, m_prev, acc)
        finish(c - 1, acc)
        return jnp.max(m8, axis=0, keepdims=True)

    m = scores_only(0)
    n_trips = (n_steps - 1) // ATTN_STEPS_PER_TRIP

    def trip(i, m):
        for j in range(ATTN_STEPS_PER_TRIP):
            m = overlapped(i * ATTN_STEPS_PER_TRIP + 1 + j, (1 + j) % 2, m)
        return m

    m = lax.fori_loop(0, n_trips, trip, m)
    for c in range(n_trips * ATTN_STEPS_PER_TRIP + 1, n_steps):
        m = overlapped(c, c % 2, m)
    values_only(n_steps - 1, (n_steps - 1) % 2, m)


def _attention(q, k, vt, lam_params, g, lam_init):
    bsz, heads, seq, _ = q.shape
    qo = pl.BlockSpec((1, heads, seq, ATTN_V_DIM), lambda b: (b, 0, 0, 0))
    vt_spec = pl.BlockSpec((1, heads, ATTN_V_DIM, seq), lambda b: (b, 0, 0, 0))
    kern = functools.partial(_attn_kernel, lam_init=lam_init)
    return pl.pallas_call(
        kern,
        grid=(bsz,),
        in_specs=[qo, qo, vt_spec, _resident(lam_params.shape), _resident(g.shape)],
        out_specs=qo,
        out_shape=jax.ShapeDtypeStruct(q.shape, BF16),
        scratch_shapes=[pltpu.VMEM((seq, 2 * ATTN_QB), F32), pltpu.VMEM((seq, 2 * ATTN_QB), F32)],
        compiler_params=_params("parallel"),
        name="attention",
    )(q, k, vt, lam_params, g)


def _cumsum_rows(x, reverse):
    n = x.shape[0]
    row = lax.broadcasted_iota(jnp.int32, x.shape, 0)
    step = 1
    while step < n:
        if reverse:
            shifted = pltpu.roll(x, n - step, 0)
            x = x + jnp.where(row < n - step, shifted, 0.0)
        else:
            shifted = pltpu.roll(x, step, 0)
            x = x + jnp.where(row >= step, shifted, 0.0)
        step *= 2
    return x


def _ssd_direction(xc_ref, dt_ref, alog_ref, state_ref, y_ref, tiles_ref, cb_ref, bt_ref, *,
                   lane0, reverse, row0):
    L = SSD_CHUNK
    n_x = SSD_HEADS * SSD_HEAD_DIM
    gw = SSD_STATE
    rows = slice(row0, row0 + L)
    dt = dt_ref[0, rows, :]
    a = -jnp.exp(alog_ref[...]) * LOG2_E
    cs = _cumsum_rows(dt * a, reverse)
    tot = cs[0:1] if reverse else cs[L - 1:L]
    w = dt * jnp.exp2(tot - cs)
    log_dt = jnp.log2(dt)
    tiles_ref[0] = cs
    tiles_ref[1] = (cs - log_dt).T
    tiles_ref[2] = log_dt.T
    tiles_ref[3] = w.T
    row = lax.broadcasted_iota(jnp.int32, (L, L), 0)
    col = lax.broadcasted_iota(jnp.int32, (L, L), 1)
    keep = (row <= col) if reverse else (row >= col)
    first = lax.broadcasted_iota(jnp.int32, (L, LANES), 1) < SSD_HEAD_DIM

    def group_cols(kind, g):
        lo = n_x + (kind * SSD_GROUPS + g) * gw
        return slice(lo, lo + gw)

    for g in range(SSD_GROUPS):
        b_g = xc_ref[0, rows, group_cols(0, g)]
        cb_ref[g] = jnp.where(keep, _dot_nt(xc_ref[0, rows, group_cols(1, g)], b_g), 0.0
                              ).astype(BF16)
        bt_ref[g] = b_g.astype(F32).T.astype(BF16)

    heads_per_group = SSD_HEADS // SSD_GROUPS
    for pair in range(SSD_HEADS // 2):
        cols = slice(pair * LANES, (pair + 1) * LANES)
        xs = xc_ref[0, rows, cols]
        prev = state_ref[:, cols]
        rhs = jnp.concatenate([xs, prev.astype(BF16)], axis=0)
        ys, ss, cds = [], [], []
        for h in (2 * pair, 2 * pair + 1):
            g = h // heads_per_group
            hl = lane0 + h
            col_cs = jnp.broadcast_to(tiles_ref[0, :, hl:hl + 1], (L, L))
            row_cs = jnp.broadcast_to(tiles_ref[1, hl:hl + 1, :], (L, L))
            row_ldt = jnp.broadcast_to(tiles_ref[2, hl:hl + 1, :], (L, L))
            decay_dt = jnp.exp2(jnp.minimum(col_cs - row_cs, row_ldt))
            m = cb_ref[g] * decay_dt.astype(BF16)
            c_e = xc_ref[0, rows, group_cols(1, g)] * jnp.exp2(col_cs).astype(BF16)
            ys.append(_dot(jnp.concatenate([m, c_e], axis=1), rhs))
            b_w = bt_ref[g] * jnp.broadcast_to(tiles_ref[3, hl:hl + 1, :], (gw, L)).astype(BF16)
            ss.append(_dot(b_w, xs))
            cds.append(jnp.broadcast_to(jnp.exp2(tot[:, hl:hl + 1]), (1, LANES)))
        y_ref[0, rows, cols] = jnp.where(first, ys[0], ys[1]).astype(y_ref.dtype)
        chunk_decay = jnp.where(first[0:1], cds[0], cds[1])
        state_ref[:, cols] = prev * chunk_decay + jnp.where(first, ss[0], ss[1])


def _ssd_kernel(xf_ref, xb_ref, dtf_ref, dtb_ref, alog_ref, yf_ref, yb_ref, sf_ref, sb_ref,
                tiles_f_ref, tiles_b_ref, cb_f_ref, cb_b_ref, bt_f_ref, bt_b_ref):
    @pl.when(pl.program_id(1) == 0)
    def _():
        sf_ref[...] = jnp.zeros_like(sf_ref)
        sb_ref[...] = jnp.zeros_like(sb_ref)

    n_sub = xf_ref.shape[1] // SSD_CHUNK
    for j in range(n_sub):
        _ssd_direction(xf_ref, dtf_ref, alog_ref, sf_ref, yf_ref, tiles_f_ref, cb_f_ref, bt_f_ref,
                       lane0=0, reverse=False, row0=j * SSD_CHUNK)
        _ssd_direction(xb_ref, dtb_ref, alog_ref, sb_ref, yb_ref, tiles_b_ref, cb_b_ref, bt_b_ref,
                       lane0=SSD_HEADS, reverse=True, row0=(n_sub - 1 - j) * SSD_CHUNK)


def _ssd(xconv, dt, alog):
    bsz, seq, n_ch = xconv.shape
    n_x = SSD_HEADS * SSD_HEAD_DIM
    rows = SSD_CHUNK * min(SSD_CHUNKS_PER_STEP, seq // SSD_CHUNK)
    nc = seq // rows
    fwd = lambda n: pl.BlockSpec((1, rows, n), lambda b, t: (b, t, 0))
    bwd = lambda n: pl.BlockSpec((1, rows, n), lambda b, t: (b, nc - 1 - t, 0))
    y_shape = jax.ShapeDtypeStruct((bsz, seq, n_x), BF16)
    return pl.pallas_call(
        _ssd_kernel,
        grid=(bsz, nc),
        in_specs=[fwd(n_ch), bwd(n_ch), fwd(LANES), bwd(LANES), _resident(alog.shape)],
        out_specs=[fwd(n_x), bwd(n_x)],
        out_shape=[y_shape, y_shape],
        scratch_shapes=[pltpu.VMEM((SSD_STATE, n_x), F32), pltpu.VMEM((SSD_STATE, n_x), F32),
                        pltpu.VMEM((4, SSD_CHUNK, LANES), F32), pltpu.VMEM((4, SSD_CHUNK, LANES), F32),
                        pltpu.VMEM((SSD_GROUPS, SSD_CHUNK, SSD_CHUNK), BF16),
                        pltpu.VMEM((SSD_GROUPS, SSD_CHUNK, SSD_CHUNK), BF16),
                        pltpu.VMEM((SSD_GROUPS, SSD_STATE, SSD_CHUNK), BF16),
                        pltpu.VMEM((SSD_GROUPS, SSD_STATE, SSD_CHUNK), BF16)],
        compiler_params=_params("parallel", "arbitrary"),
        name="ssd",
    )(xconv, xconv, dt, dt, alog)


def _out_ffn_kernel(h_ref, attn_ref, yf_ref, yb_ref, xs_ref, z_ref, dskip_ref, gssd_ref,
                    wo_ref, gmix_ref, gpre_ref, wg_ref, wu_ref, wd_ref, gpost_ref,
                    gfinal_ref, o_ref):
    heads = attn_ref.shape[1]
    attn = jnp.concatenate([attn_ref[0, h] for h in range(heads)], axis=1)
    n_attn = attn.shape[-1]
    y = (yf_ref[...].astype(F32) + yb_ref[...].astype(F32)
         + xs_ref[...].astype(F32) * dskip_ref[...])
    y = _rms(y * _silu(z_ref[...].astype(F32)), gssd_ref[...])
    m = _dot(attn, wo_ref[:n_attn, :]) + _dot(y.astype(BF16), wo_ref[n_attn:, :])
    h = h_ref[...] + _rms(m, gmix_ref[...])
    h = _swiglu_half_step(h, gpre_ref[...], wg_ref, wu_ref, wd_ref, gpost_ref[...])
    o_ref[...] = _rms(h, gfinal_ref[...])


def _out_ffn(h2d, attn, yf, yb, xconv, z, dskip, gssd, wo, gmix, gpre, wg, wu, wd, gpost,
             gfinal, tm):
    t, d = h2d.shape
    _, heads, seq, v_dim = attn.shape
    nb = seq // tm
    attn_spec = pl.BlockSpec((1, heads, tm, v_dim), lambda i: (i // nb, 0, i % nb, 0))
    f = wg.shape[1]
    n_x = yf.shape[-1]
    row = lambda n: pl.BlockSpec((tm, n), lambda i: (i, 0))
    return pl.pallas_call(
        _out_ffn_kernel,
        grid=(t // tm,),
        in_specs=[row(d), attn_spec, row(n_x), row(n_x), row(n_x), row(n_x),
                  _resident((1, n_x)), _resident((1, n_x)), _resident(wo.shape),
                  _resident((1, d)), _resident((1, d)), _resident((d, f)),
                  _resident((d, f)), _resident((f, d)), _resident((1, d)),
                  _resident((1, d))],
        out_specs=row(d),
        out_shape=jax.ShapeDtypeStruct((t, d), F32),
        compiler_params=_params("parallel"),
        name="out_ffn",
    )(h2d, attn, yf, yb, xconv, z, dskip, gssd, wo, gmix, gpre, wg, wu, wd, gpost, gfinal)


def _rotary_tables(seq):
    half = ROT_DIM // 2
    pos = jnp.arange(seq, dtype=F32)
    inv_freq = jnp.power(F32(ROPE_THETA), -jnp.arange(0, ROT_DIM, 2, dtype=F32) / ROT_DIM)
    ang = pos[:, None] * inv_freq[None, :]
    cos, sin = jnp.cos(ang), jnp.sin(ang)
    ones = jnp.ones((seq, ATTN_QK_DIM - ROT_DIM), F32)
    zeros_h = jnp.zeros((seq, half), F32)
    zeros_r = jnp.zeros((seq, ATTN_QK_DIM - ROT_DIM), F32)
    reps = LANES // ATTN_QK_DIM
    cos_t = jnp.tile(jnp.concatenate([cos, cos, ones], axis=1), (1, reps))
    sa_t = jnp.tile(jnp.concatenate([-sin, zeros_h, zeros_r], axis=1), (1, reps))
    sb_t = jnp.tile(jnp.concatenate([zeros_h, sin, zeros_r], axis=1), (1, reps))
    return cos_t, sa_t, sb_t


def _pad_lanes(v, width=LANES):
    return jnp.pad(v, ((0, 0), (0, width - v.shape[-1])))


def _layer(h2d, layer_idx, p, bsz, seq):
    t, d = h2d.shape
    tm = min(ROW_TILE, seq)
    row = lambda v: v.reshape(1, -1).astype(F32)
    bf = lambda w: w.astype(BF16)

    h1 = _ffn(h2d, row(p["ffn1_pre_g"]), bf(p["ffn1_w_gate"]), bf(p["ffn1_w_up"]),
              bf(p["ffn1_w_down"]), row(p["ffn1_post_g"]), tm)

    qk_cols = ATTN_HEADS * 2 * ATTN_QK_DIM
    v_cols = ATTN_HEADS * ATTN_V_DIM
    n_x = SSD_HEADS * SSD_HEAD_DIM
    xbc_cols = n_x + 2 * SSD_GROUPS * SSD_STATE
    w_in = p["w_in"]
    v0 = 2 * qk_cols
    wvt = bf(w_in[:, v0:v0 + v_cols].T)
    n_dt = w_in.shape[1] - (v0 + v_cols + n_x + xbc_cols)
    w_pad = bf(jnp.pad(jnp.concatenate([w_in[:, :v0], w_in[:, v0 + v_cols:]], axis=1),
                       ((0, 0), (0, LANES - n_dt))))
    dt_bias = _pad_lanes(jnp.concatenate([p["dt_bias_fwd"], p["dt_bias_bwd"]]).reshape(1, -1)
                         .astype(F32))
    alog = _pad_lanes(jnp.concatenate([p["a_log_fwd"], p["a_log_bwd"]]).reshape(1, -1)
                      .astype(F32))
    cos_t, sa_t, sb_t = _rotary_tables(seq)
    q, k, vt, z, xconv, dt = _in_proj(h1, row(p["mix_pre_g"]), w_pad, wvt, cos_t, sa_t, sb_t,
                                      dt_bias, p["conv_w"].astype(F32), row(p["conv_b"]), tm, seq,
                                      qk_cols, n_x, xbc_cols)
    shape3 = lambda a: a.reshape(bsz, seq, a.shape[-1])
    xconv = shape3(xconv)

    lam_init = 0.8 - 0.6 * math.exp(-0.3 * layer_idx)
    lam_params = jnp.stack([p["lambda_q1"], p["lambda_k1"], p["lambda_q2"],
                            p["lambda_k2"]]).astype(F32)
    attn = _attention(q, k, vt, lam_params, row(p["attn_subln_g"]), lam_init)

    yf, yb = _ssd(xconv, shape3(dt), alog)

    dskip = jnp.repeat(p["d_skip"].astype(F32), SSD_HEAD_DIM).reshape(1, -1)
    flat = lambda a: a.reshape(t, a.shape[-1])
    return _out_ffn(h1, attn, flat(yf), flat(yb), flat(xconv), z, dskip,
                    row(p["ssd_norm_g"]), bf(p["w_out"]), row(p["mix_post_g"]),
                    row(p["ffn2_pre_g"]), bf(p["ffn2_w_gate"]), bf(p["ffn2_w_up"]),
                    bf(p["ffn2_w_down"]), row(p["ffn2_post_g"]), row(p["final_g"]), tm)


def kernel(x, ffn1_pre_g, ffn1_w_gate, ffn1_w_up, ffn1_w_down, ffn1_post_g, mix_pre_g, w_in, lambda_q1, lambda_k1, lambda_q2, lambda_k2, attn_subln_g, conv_w, conv_b, a_log_fwd, a_log_bwd, dt_bias_fwd, dt_bias_bwd, d_skip, ssd_norm_g, w_out, mix_post_g, ffn2_pre_g, ffn2_w_gate, ffn2_w_up, ffn2_w_down, ffn2_post_g, final_g):
    names = ("ffn1_pre_g", "ffn1_w_gate", "ffn1_w_up", "ffn1_w_down", "ffn1_post_g",
             "mix_pre_g", "w_in", "lambda_q1", "lambda_k1", "lambda_q2", "lambda_k2",
             "attn_subln_g", "conv_w", "conv_b", "a_log_fwd", "a_log_bwd", "dt_bias_fwd",
             "dt_bias_bwd", "d_skip", "ssd_norm_g", "w_out", "mix_post_g", "ffn2_pre_g",
             "ffn2_w_gate", "ffn2_w_up", "ffn2_w_down", "ffn2_post_g", "final_g")
    stacked = dict(zip(names, (ffn1_pre_g, ffn1_w_gate, ffn1_w_up, ffn1_w_down, ffn1_post_g,
                               mix_pre_g, w_in, lambda_q1, lambda_k1, lambda_q2, lambda_k2,
                               attn_subln_g, conv_w, conv_b, a_log_fwd, a_log_bwd,
                               dt_bias_fwd, dt_bias_bwd, d_skip, ssd_norm_g, w_out,
                               mix_post_g, ffn2_pre_g, ffn2_w_gate, ffn2_w_up, ffn2_w_down,
                               ffn2_post_g, final_g)))
    bsz, seq, d = x.shape
    h = x.reshape(bsz * seq, d)
    for i in range(ffn1_pre_g.shape[0]):
        h = _layer(h, i, {n: a[i] for n, a in stacked.items()}, bsz, seq)
    return h.reshape(bsz, seq, d)
```

```python
import functools
import math

import jax
import jax.numpy as jnp
from jax import lax
from jax.experimental import pallas as pl
from jax.experimental.pallas import tpu as pltpu

F32 = jnp.float32
BF16 = jnp.bfloat16

EPS = 1e-6
ATTN_HEADS = 8
ATTN_QK_DIM = 64
ATTN_V_DIM = 2 * ATTN_QK_DIM
ROT_DIM = ATTN_QK_DIM // 4
ROPE_THETA = 500000.0
SSD_HEADS = 16
SSD_HEAD_DIM = 64
SSD_GROUPS = 2
SSD_STATE = 128
SSD_CONV = 5
SSD_CHUNK = 128
SSD_CHUNKS_PER_STEP = 8

LANES = 128
MXU_COLS = 256
BF16_ROWS = 16
LOG2_E = math.log2(math.e)
ATTN_QB = MXU_COLS
ATTN_KB = MXU_COLS
ATTN_STEPS_PER_TRIP = 16
ROW_TILE = 512
FFN_ROW_GROUPS = 2
CONV_ROWS = 64
SUBLANES = 8
VMEM_LIMIT_BYTES = 56 * 1024 * 1024


def _params(*semantics):
    return pltpu.CompilerParams(dimension_semantics=semantics,
                                vmem_limit_bytes=VMEM_LIMIT_BYTES)


def _resident(shape):
    zeros = (0,) * len(shape)
    return pl.BlockSpec(shape, lambda *_: zeros, pipeline_mode=pl.Buffered(1))


def _rms(x, g):
    return x * lax.rsqrt(jnp.mean(x * x, axis=-1, keepdims=True) + EPS) * g


def _silu(x):
    return x * jax.nn.sigmoid(x)


def _dot(a, b):
    return jnp.dot(a, b, preferred_element_type=F32)


def _dot_nt(a, b):
    return lax.dot_general(a, b, (((1,), (1,)), ((), ())), preferred_element_type=F32)


def _swiglu_half_step(x, gpre, wg_ref, wu_ref, wd_ref, gpost):
    outs = []
    rows = x.shape[0] // FFN_ROW_GROUPS
    xn = _rms(x, gpre).astype(BF16)
    acts = []
    for r in range(FFN_ROW_GROUPS):
        xr = xn[r * rows:(r + 1) * rows]
        acts.append((_silu(_dot(xr, wg_ref[...])) * _dot(xr, wu_ref[...])).astype(BF16))
    for r in range(FFN_ROW_GROUPS):
        f = _dot(acts[r], wd_ref[...])
        outs.append(x[r * rows:(r + 1) * rows] + 0.5 * _rms(f, gpost))
    return jnp.concatenate(outs, axis=0)


def _ffn_kernel(x_ref, gpre_ref, wg_ref, wu_ref, wd_ref, gpost_ref, o_ref):
    o_ref[...] = _swiglu_half_step(x_ref[...], gpre_ref[...], wg_ref, wu_ref, wd_ref,
                                   gpost_ref[...])


def _ffn(x2d, gpre, wg, wu, wd, gpost, tm):
    t, d = x2d.shape
    f = wg.shape[1]
    row = pl.BlockSpec((tm, d), lambda i: (i, 0))
    return pl.pallas_call(
        _ffn_kernel,
        grid=(t // tm,),
        in_specs=[row, _resident((1, d)), _resident((d, f)), _resident((d, f)),
                  _resident((f, d)), _resident((1, d))],
        out_specs=row,
        out_shape=jax.ShapeDtypeStruct((t, d), F32),
        compiler_params=_params("parallel"),
        name="ffn1",
    )(x2d, gpre, wg, wu, wd, gpost)


def _in_proj_kernel(hp_ref, h_ref, hx_ref, g_ref, w_ref, wvt_ref, cos_ref, sa_ref, sb_ref,
                    dtb_ref, cw_ref, cb_ref, q_ref, k_ref, vt_ref, z_ref, xc_ref, dt_ref, *,
                    qk_cols, z_cols, xbc_cols, nb):
    tm = h_ref.shape[0]
    hn = _rms(h_ref[...], g_ref[...]).astype(BF16)
    hn_ext = jnp.concatenate([_rms(hp_ref[...], g_ref[...]).astype(BF16), hn,
                              _rms(hx_ref[...], g_ref[...]).astype(BF16)], axis=0)
    i = pl.program_id(0) % nb
    row = lax.broadcasted_iota(jnp.int32, (tm + 2 * SUBLANES, 1), 0)
    inside = jnp.logical_and(jnp.logical_or(i > 0, row >= SUBLANES),
                             jnp.logical_or(i < nb - 1, row < tm + SUBLANES))
    pad = SSD_CONV // 2
    c_z = 2 * qk_cols
    c_x = c_z + z_cols
    c_dt = c_x + xbc_cols
    cos = cos_ref[...]
    sa = sa_ref[...]
    sb = sb_ref[...]

    def conv_dot(c):
        ext = _dot(hn_ext, w_ref[:, c_x + c * MXU_COLS:c_x + (c + 1) * MXU_COLS])
        return jnp.where(inside, ext, 0.0)

    def conv_taps(c, ext):
        cols = slice(c * MXU_COLS, (c + 1) * MXU_COLS)
        for r0 in range(0, tm, CONV_ROWS):
            piece = ext[r0:r0 + CONV_ROWS + 2 * SUBLANES]
            acc = jnp.zeros((CONV_ROWS, MXU_COLS), F32) + cb_ref[:, cols]
            for k in range(SSD_CONV):
                off = SUBLANES - pad + k
                acc = acc + piece[off:off + CONV_ROWS] * cw_ref[k:k + 1, cols]
            xc_ref[r0:r0 + CONV_ROWS, cols] = _silu(acc).astype(xc_ref.dtype)

    def rotary_block(col0, out_ref, mult, c):
        t2 = _dot(hn, w_ref[:, col0 + c * MXU_COLS:col0 + (c + 1) * MXU_COLS])
        for half in range(MXU_COLS // LANES):
            t = t2[:, half * LANES:(half + 1) * LANES]
            r = t * cos + pltpu.roll(t, LANES - ROT_DIM // 2, 1) * sa \
                + pltpu.roll(t, ROT_DIM // 2, 1) * sb
            out_ref[0, c * (MXU_COLS // LANES) + half] = (r * mult).astype(out_ref.dtype)

    def z_block(c):
        cols = slice(c * MXU_COLS, (c + 1) * MXU_COLS)
        z_ref[:, cols] = _dot(hn, w_ref[:, c_z + c * MXU_COLS:c_z + (c + 1) * MXU_COLS]
                              ).astype(z_ref.dtype)

    n_conv = xbc_cols // MXU_COLS
    n_rot = qk_cols // MXU_COLS
    units = ([functools.partial(rotary_block, 0, q_ref, ATTN_QK_DIM ** -0.5 * LOG2_E, c)
              for c in range(n_rot)]
             + [functools.partial(rotary_block, qk_cols, k_ref, 1.0, c) for c in range(n_rot)]
             + [functools.partial(z_block, c) for c in range(z_cols // MXU_COLS)])
    per_conv = -(-len(units) // n_conv)
    for c in range(n_conv):
        ext = conv_dot(c)
        for unit in units[c * per_conv:(c + 1) * per_conv]:
            unit()
        conv_taps(c, ext)
    vt_ref[0] = _dot_nt(wvt_ref[...], hn).astype(vt_ref.dtype).reshape(vt_ref.shape[1:])
    dt_raw = _dot(hn, w_ref[:, c_dt:c_dt + LANES]) + dtb_ref[...]
    dt_ref[...] = jax.nn.softplus(dt_raw)


def _in_proj(h2d, g, w_pad, wvt, cos, sa, sb, dtb, conv_w, conv_b, tm, seq, qk_cols, z_cols,
             xbc_cols):
    t, d = h2d.shape
    per = tm // SUBLANES
    n_halo = t // SUBLANES
    prev = pl.BlockSpec((SUBLANES, d), lambda i: (jnp.maximum(i * per - 1, 0), 0))
    nxt = pl.BlockSpec((SUBLANES, d), lambda i: (jnp.minimum((i + 1) * per, n_halo - 1), 0))
    v_cols = wvt.shape[0]
    nb = seq // tm
    row = lambda n: pl.BlockSpec((tm, n), lambda i: (i, 0))
    pos = pl.BlockSpec((tm, LANES), lambda i: (i % nb, 0))
    heads = qk_cols // LANES
    qk_spec = pl.BlockSpec((1, heads, tm, LANES), lambda i: (i // nb, 0, i % nb, 0))
    vt_spec = pl.BlockSpec((1, heads, v_cols // heads, tm), lambda i: (i // nb, 0, 0, i % nb))
    kern = functools.partial(_in_proj_kernel, qk_cols=qk_cols, z_cols=z_cols,
                             xbc_cols=xbc_cols, nb=nb)
    return pl.pallas_call(
        kern,
        grid=(t // tm,),
        in_specs=[prev, row(d), nxt, _resident((1, d)), _resident(w_pad.shape),
                  _resident(wvt.shape), pos, pos, pos, _resident((1, LANES)),
                  _resident(conv_w.shape), _resident(conv_b.shape)],
        out_specs=[qk_spec, qk_spec, vt_spec, row(z_cols), row(xbc_cols), row(LANES)],
        out_shape=[jax.ShapeDtypeStruct((t // seq, heads, seq, LANES), BF16),
                   jax.ShapeDtypeStruct((t // seq, heads, seq, LANES), BF16),
                   jax.ShapeDtypeStruct((t // seq, heads, v_cols // heads, seq), BF16),
                   jax.ShapeDtypeStruct((t, z_cols), BF16),
                   jax.ShapeDtypeStruct((t, xbc_cols), BF16),
                   jax.ShapeDtypeStruct((t, LANES), F32)],
        compiler_params=_params("parallel"),
        name="in_proj",
    )(h2d, h2d, h2d, g, w_pad, wvt, cos, sa, sb, dtb, conv_w, conv_b)


def _attn_kernel(q_ref, k_ref, vt_ref, lam_ref, g_ref, o_ref, st_a_ref, st_b_ref, *, lam_init):
    heads, seq = k_ref.shape[1], k_ref.shape[2]
    n_q = seq // ATTN_QB
    n_kb = seq // ATTN_KB
    n_steps = heads * n_q
    lp = lam_ref[...]
    lam = (jnp.exp(jnp.sum(lp[0:1] * lp[1:2], axis=-1, keepdims=True))
           - jnp.exp(jnp.sum(lp[2:3] * lp[3:4], axis=-1, keepdims=True)) + lam_init)
    first = lax.broadcasted_iota(jnp.int32, (ATTN_V_DIM, ATTN_QB), 0) < ATTN_QK_DIM
    ones_rows = jnp.ones((BF16_ROWS, ATTN_KB), BF16)

    def query_rows(c):
        return pl.ds(pl.multiple_of((c % n_q) * ATTN_QB, ATTN_QB), ATTN_QB)

    def both_maps(c):
        qt = q_ref[0, c // n_q, query_rows(c), :].astype(F32).T.astype(BF16)
        zero = jnp.zeros_like(qt)
        return jnp.concatenate([jnp.where(first, qt, zero), jnp.where(first, zero, qt)], axis=1)

    def scores_block(c, q2, st_ref, kb, m8):
        keys = slice(kb * ATTN_KB, (kb + 1) * ATTN_KB)
        st = _dot(k_ref[0, c // n_q, keys, :], q2)
        st_ref[keys, :] = st
        return jnp.maximum(m8, jnp.max(st.reshape(ATTN_KB // 8, 8, 2 * ATTN_QB), axis=0))

    def values_block(c, st_ref, kb, m, acc):
        keys = slice(kb * ATTN_KB, (kb + 1) * ATTN_KB)
        p = jnp.exp2(st_ref[keys, :] - m).astype(BF16)
        vt_aug = jnp.concatenate([vt_ref[0, c // n_q, :, keys], ones_rows], axis=0)
        return acc + _dot(vt_aug, p)

    def finish(c, acc):
        o1 = acc[:ATTN_V_DIM, :ATTN_QB] * (1.0 / acc[ATTN_V_DIM:ATTN_V_DIM + 1, :ATTN_QB])
        o2 = acc[:ATTN_V_DIM, ATTN_QB:] * (1.0 / acc[ATTN_V_DIM:ATTN_V_DIM + 1, ATTN_QB:])
        o = (o1 - lam * o2).T
        o_ref[0, c // n_q, query_rows(c), :] = (
            _rms(o, g_ref[...]) * (1.0 - lam_init)).astype(o_ref.dtype)

    m8_init = jnp.full((8, 2 * ATTN_QB), -jnp.inf, F32)
    acc_init = jnp.zeros((ATTN_V_DIM + BF16_ROWS, 2 * ATTN_QB), F32)
    buffers = (st_a_ref, st_b_ref)

    def scores_only(c):
        q2 = both_maps(c)
        m8 = m8_init
        for kb in range(n_kb):
            m8 = scores_block(c, q2, buffers[0], kb, m8)
        return jnp.max(m8, axis=0, keepdims=True)

    def values_only(c, parity, m):
        acc = acc_init
        for kb in range(n_kb):
            acc = values_block(c, buffers[parity], kb, m, acc)
        finish(c, acc)

    def overlapped(c, parity, m_prev):
        q2 = both_maps(c)
        m8, acc = m8_init, acc_init
        for kb in range(n_kb):
            m8 = scores_block(c, q2, buffers[parity], kb, m8)
            acc = values_block(c - 1, buffers[1 - parity], kb, m_prev, acc)
        finish(c - 1, acc)
        return jnp.max(m8, axis=0, keepdims=True)

    m = scores_only(0)
    n_trips = (n_steps - 1) // ATTN_STEPS_PER_TRIP

    def trip(i, m):
        for j in range(ATTN_STEPS_PER_TRIP):
            m = overlapped(i * ATTN_STEPS_PER_TRIP + 1 + j, (1 + j) % 2, m)
        return m

    m = lax.fori_loop(0, n_trips, trip, m)
    for c in range(n_trips * ATTN_STEPS_PER_TRIP + 1, n_steps):
        m = overlapped(c, c % 2, m)
    values_only(n_steps - 1, (n_steps - 1) % 2, m)


def _attention(q, k, vt, lam_params, g, lam_init):
    bsz, heads, seq, _ = q.shape
    qo = pl.BlockSpec((1, heads, seq, ATTN_V_DIM), lambda b: (b, 0, 0, 0))
    vt_spec = pl.BlockSpec((1, heads, ATTN_V_DIM, seq), lambda b: (b, 0, 0, 0))
    kern = functools.partial(_attn_kernel, lam_init=lam_init)
    return pl.pallas_call(
        kern,
        grid=(bsz,),
        in_specs=[qo, qo, vt_spec, _resident(lam_params.shape), _resident(g.shape)],
        out_specs=qo,
        out_shape=jax.ShapeDtypeStruct(q.shape, BF16),
        scratch_shapes=[pltpu.VMEM((seq, 2 * ATTN_QB), F32), pltpu.VMEM((seq, 2 * ATTN_QB), F32)],
        compiler_params=_params("parallel"),
        name="attention",
    )(q, k, vt, lam_params, g)


def _cumsum_rows(x, reverse):
    n = x.shape[0]
    row = lax.broadcasted_iota(jnp.int32, x.shape, 0)
    step = 1
    while step < n:
        if reverse:
            shifted = pltpu.roll(x, n - step, 0)
            x = x + jnp.where(row < n - step, shifted, 0.0)
        else:
            shifted = pltpu.roll(x, step, 0)
            x = x + jnp.where(row >= step, shifted, 0.0)
        step *= 2
    return x


def _ssd_direction(xc_ref, dt_ref, alog_ref, state_ref, y_ref, tiles_ref, cb_ref, bt_ref, *,
                   lane0, reverse, row0):
    L = SSD_CHUNK
    n_x = SSD_HEADS * SSD_HEAD_DIM
    gw = SSD_STATE
    rows = slice(row0, row0 + L)
    dt = dt_ref[0, rows, :]
    a = -jnp.exp(alog_ref[...]) * LOG2_E
    cs = _cumsum_rows(dt * a, reverse)
    tot = cs[0:1] if reverse else cs[L - 1:L]
    w = dt * jnp.exp2(tot - cs)
    log_dt = jnp.log2(dt)
    tiles_ref[0] = cs
    tiles_ref[1] = (cs - log_dt).T
    tiles_ref[2] = log_dt.T
    tiles_ref[3] = w.T
    row = lax.broadcasted_iota(jnp.int32, (L, L), 0)
    col = lax.broadcasted_iota(jnp.int32, (L, L), 1)
    keep = (row <= col) if reverse else (row >= col)
    first = lax.broadcasted_iota(jnp.int32, (L, LANES), 1) < SSD_HEAD_DIM

    def group_cols(kind, g):
        lo = n_x + (kind * SSD_GROUPS + g) * gw
        return slice(lo, lo + gw)

    for g in range(SSD_GROUPS):
        b_g = xc_ref[0, rows, group_cols(0, g)]
        cb_ref[g] = jnp.where(keep, _dot_nt(xc_ref[0, rows, group_cols(1, g)], b_g), 0.0
                              ).astype(BF16)
        bt_ref[g] = b_g.astype(F32).T.astype(BF16)

    heads_per_group = SSD_HEADS // SSD_GROUPS
    for pair in range(SSD_HEADS // 2):
        cols = slice(pair * LANES, (pair + 1) * LANES)
        xs = xc_ref[0, rows, cols]
        prev = state_ref[:, cols]
        rhs = jnp.concatenate([xs, prev.astype(BF16)], axis=0)
        ys, ss, cds = [], [], []
        for h in (2 * pair, 2 * pair + 1):
            g = h // heads_per_group
            hl = lane0 + h
            col_cs = jnp.broadcast_to(tiles_ref[0, :, hl:hl + 1], (L, L))
            row_cs = jnp.broadcast_to(tiles_ref[1, hl:hl + 1, :], (L, L))
            row_ldt = jnp.broadcast_to(tiles_ref[2, hl:hl + 1, :], (L, L))
            decay_dt = jnp.exp2(jnp.minimum(col_cs - row_cs, row_ldt))
            m = cb_ref[g] * decay_dt.astype(BF16)
            c_e = xc_ref[0, rows, group_cols(1, g)] * jnp.exp2(col_cs).astype(BF16)
            ys.append(_dot(jnp.concatenate([m, c_e], axis=1), rhs))
            b_w = bt_ref[g] * jnp.broadcast_to(tiles_ref[3, hl:hl + 1, :], (gw, L)).astype(BF16)
            ss.append(_dot(b_w, xs))
            cds.append(jnp.broadcast_to(jnp.exp2(tot[:, hl:hl + 1]), (1, LANES)))
        y_ref[0, rows, cols] = jnp.where(first, ys[0], ys[1]).astype(y_ref.dtype)
        chunk_decay = jnp.where(first[0:1], cds[0], cds[1])
        state_ref[:, cols] = prev * chunk_decay + jnp.where(first, ss[0], ss[1])


def _ssd_kernel(xf_ref, xb_ref, dtf_ref, dtb_ref, alog_ref, yf_ref, yb_ref, sf_ref, sb_ref,
                tiles_f_ref, tiles_b_ref, cb_f_ref, cb_b_ref, bt_f_ref, bt_b_ref):
    @pl.when(pl.program_id(1) == 0)
    def _():
        sf_ref[...] = jnp.zeros_like(sf_ref)
        sb_ref[...] = jnp.zeros_like(sb_ref)

    n_sub = xf_ref.shape[1] // SSD_CHUNK
    for j in range(n_sub):
        _ssd_direction(xf_ref, dtf_ref, alog_ref, sf_ref, yf_ref, tiles_f_ref, cb_f_ref, bt_f_ref,
                       lane0=0, reverse=False, row0=j * SSD_CHUNK)
        _ssd_direction(xb_ref, dtb_ref, alog_ref, sb_ref, yb_ref, tiles_b_ref, cb_b_ref, bt_b_ref,
                       lane0=SSD_HEADS, reverse=True, row0=(n_sub - 1 - j) * SSD_CHUNK)


def _ssd(xconv, dt, alog):
    bsz, seq, n_ch = xconv.shape
    n_x = SSD_HEADS * SSD_HEAD_DIM
    rows = SSD_CHUNK * min(SSD_CHUNKS_PER_STEP, seq // SSD_CHUNK)
    nc = seq // rows
    fwd = lambda n: pl.BlockSpec((1, rows, n), lambda b, t: (b, t, 0))
    bwd = lambda n: pl.BlockSpec((1, rows, n), lambda b, t: (b, nc - 1 - t, 0))
    y_shape = jax.ShapeDtypeStruct((bsz, seq, n_x), BF16)
    return pl.pallas_call(
        _ssd_kernel,
        grid=(bsz, nc),
        in_specs=[fwd(n_ch), bwd(n_ch), fwd(LANES), bwd(LANES), _resident(alog.shape)],
        out_specs=[fwd(n_x), bwd(n_x)],
        out_shape=[y_shape, y_shape],
        scratch_shapes=[pltpu.VMEM((SSD_STATE, n_x), F32), pltpu.VMEM((SSD_STATE, n_x), F32),
                        pltpu.VMEM((4, SSD_CHUNK, LANES), F32), pltpu.VMEM((4, SSD_CHUNK, LANES), F32),
                        pltpu.VMEM((SSD_GROUPS, SSD_CHUNK, SSD_CHUNK), BF16),
                        pltpu.VMEM((SSD_GROUPS, SSD_CHUNK, SSD_CHUNK), BF16),
                        pltpu.VMEM((SSD_GROUPS, SSD_STATE, SSD_CHUNK), BF16),
                        pltpu.VMEM((SSD_GROUPS, SSD_STATE, SSD_CHUNK), BF16)],
        compiler_params=_params("parallel", "arbitrary"),
        name="ssd",
    )(xconv, xconv, dt, dt, alog)


def _out_ffn_kernel(h_ref, attn_ref, yf_ref, yb_ref, xs_ref, z_ref, dskip_ref, gssd_ref,
                    wo_ref, gmix_ref, gpre_ref, wg_ref, wu_ref, wd_ref, gpost_ref,
                    gfinal_ref, o_ref):
    heads = attn_ref.shape[1]
    attn = jnp.concatenate([attn_ref[0, h] for h in range(heads)], axis=1)
    n_attn = attn.shape[-1]
    y = (yf_ref[...].astype(F32) + yb_ref[...].astype(F32)
         + xs_ref[...].astype(F32) * dskip_ref[...])
    y = _rms(y * _silu(z_ref[...].astype(F32)), gssd_ref[...])
    m = _dot(attn, wo_ref[:n_attn, :]) + _dot(y.astype(BF16), wo_ref[n_attn:, :])
    h = h_ref[...] + _rms(m, gmix_ref[...])
    h = _swiglu_half_step(h, gpre_ref[...], wg_ref, wu_ref, wd_ref, gpost_ref[...])
    o_ref[...] = _rms(h, gfinal_ref[...])


def _out_ffn(h2d, attn, yf, yb, xconv, z, dskip, gssd, wo, gmix, gpre, wg, wu, wd, gpost,
             gfinal, tm):
    t, d = h2d.shape
    _, heads, seq, v_dim = attn.shape
    nb = seq // tm
    attn_spec = pl.BlockSpec((1, heads, tm, v_dim), lambda i: (i // nb, 0, i % nb, 0))
    f = wg.shape[1]
    n_x = yf.shape[-1]
    row = lambda n: pl.BlockSpec((tm, n), lambda i: (i, 0))
    return pl.pallas_call(
        _out_ffn_kernel,
        grid=(t // tm,),
        in_specs=[row(d), attn_spec, row(n_x), row(n_x), row(n_x), row(n_x),
                  _resident((1, n_x)), _resident((1, n_x)), _resident(wo.shape),
                  _resident((1, d)), _resident((1, d)), _resident((d, f)),
                  _resident((d, f)), _resident((f, d)), _resident((1, d)),
                  _resident((1, d))],
        out_specs=row(d),
        out_shape=jax.ShapeDtypeStruct((t, d), F32),
        compiler_params=_params("parallel"),
        name="out_ffn",
    )(h2d, attn, yf, yb, xconv, z, dskip, gssd, wo, gmix, gpre, wg, wu, wd, gpost, gfinal)


def _rotary_tables(seq):
    half = ROT_DIM // 2
    pos = jnp.arange(seq, dtype=F32)
    inv_freq = jnp.power(F32(ROPE_THETA), -jnp.arange(0, ROT_DIM, 2, dtype=F32) / ROT_DIM)
    ang = pos[:, None] * inv_freq[None, :]
    cos, sin = jnp.cos(ang), jnp.sin(ang)
    ones = jnp.ones((seq, ATTN_QK_DIM - ROT_DIM), F32)
    zeros_h = jnp.zeros((seq, half), F32)
    zeros_r = jnp.zeros((seq, ATTN_QK_DIM - ROT_DIM), F32)
    reps = LANES // ATTN_QK_DIM
    cos_t = jnp.tile(jnp.concatenate([cos, cos, ones], axis=1), (1, reps))
    sa_t = jnp.tile(jnp.concatenate([-sin, zeros_h, zeros_r], axis=1), (1, reps))
    sb_t = jnp.tile(jnp.concatenate([zeros_h, sin, zeros_r], axis=1), (1, reps))
    return cos_t, sa_t, sb_t


def _pad_lanes(v, width=LANES):
    return jnp.pad(v, ((0, 0), (0, width - v.shape[-1])))


def _layer(h2d, layer_idx, p, bsz, seq):
    t, d = h2d.shape
    tm = min(ROW_TILE, seq)
    row = lambda v: v.reshape(1, -1).astype(F32)
    bf = lambda w: w.astype(BF16)

    h1 = _ffn(h2d, row(p["ffn1_pre_g"]), bf(p["ffn1_w_gate"]), bf(p["ffn1_w_up"]),
              bf(p["ffn1_w_down"]), row(p["ffn1_post_g"]), tm)

    qk_cols = ATTN_HEADS * 2 * ATTN_QK_DIM
    v_cols = ATTN_HEADS * ATTN_V_DIM
    n_x = SSD_HEADS * SSD_HEAD_DIM
    xbc_cols = n_x + 2 * SSD_GROUPS * SSD_STATE
    w_in = p["w_in"]
    v0 = 2 * qk_cols
    wvt = bf(w_in[:, v0:v0 + v_cols].T)
    n_dt = w_in.shape[1] - (v0 + v_cols + n_x + xbc_cols)
    w_pad = bf(jnp.pad(jnp.concatenate([w_in[:, :v0], w_in[:, v0 + v_cols:]], axis=1),
                       ((0, 0), (0, LANES - n_dt))))
    dt_bias = _pad_lanes(jnp.concatenate([p["dt_bias_fwd"], p["dt_bias_bwd"]]).reshape(1, -1)
                         .astype(F32))
    alog = _pad_lanes(jnp.concatenate([p["a_log_fwd"], p["a_log_bwd"]]).reshape(1, -1)
                      .astype(F32))
    cos_t, sa_t, sb_t = _rotary_tables(seq)
    q, k, vt, z, xconv, dt = _in_proj(h1, row(p["mix_pre_g"]), w_pad, wvt, cos_t, sa_t, sb_t,
                                      dt_bias, p["conv_w"].astype(F32), row(p["conv_b"]), tm, seq,
                                      qk_cols, n_x, xbc_cols)
    shape3 = lambda a: a.reshape(bsz, seq, a.shape[-1])
    xconv = shape3(xconv)

    lam_init = 0.8 - 0.6 * math.exp(-0.3 * layer_idx)
    lam_params = jnp.stack([p["lambda_q1"], p["lambda_k1"], p["lambda_q2"],
                            p["lambda_k2"]]).astype(F32)
    attn = _attention(q, k, vt, lam_params, row(p["attn_subln_g"]), lam_init)

    yf, yb = _ssd(xconv, shape3(dt), alog)

    dskip = jnp.repeat(p["d_skip"].astype(F32), SSD_HEAD_DIM).reshape(1, -1)
    flat = lambda a: a.reshape(t, a.shape[-1])
    return _out_ffn(h1, attn, flat(yf), flat(yb), flat(xconv), z, dskip,
                    row(p["ssd_norm_g"]), bf(p["w_out"]), row(p["mix_post_g"]),
                    row(p["ffn2_pre_g"]), bf(p["ffn2_w_gate"]), bf(p["ffn2_w_up"]),
                    bf(p["ffn2_w_down"]), row(p["ffn2_post_g"]), row(p["final_g"]), tm)


def kernel(x, ffn1_pre_g, ffn1_w_gate, ffn1_w_up, ffn1_w_down, ffn1_post_g, mix_pre_g, w_in, lambda_q1, lambda_k1, lambda_q2, lambda_k2, attn_subln_g, conv_w, conv_b, a_log_fwd, a_log_bwd, dt_bias_fwd, dt_bias_bwd, d_skip, ssd_norm_g, w_out, mix_post_g, ffn2_pre_g, ffn2_w_gate, ffn2_w_up, ffn2_w_down, ffn2_post_g, final_g):
    names = ("ffn1_pre_g", "ffn1_w_gate", "ffn1_w_up", "ffn1_w_down", "ffn1_post_g",
             "mix_pre_g", "w_in", "lambda_q1", "lambda_k1", "lambda_q2", "lambda_k2",
             "attn_subln_g", "conv_w", "conv_b", "a_log_fwd", "a_log_bwd", "dt_bias_fwd",
             "dt_bias_bwd", "d_skip", "ssd_norm_g", "w_out", "mix_post_g", "ffn2_pre_g",
             "ffn2_w_gate", "ffn2_w_up", "ffn2_w_down", "ffn2_post_g", "final_g")
    stacked = dict(zip(names, (ffn1_pre_g, ffn1_w_gate, ffn1_w_up, ffn1_w_down, ffn1_post_g,
                               mix_pre_g, w_in, lambda_q1, lambda_k1, lambda_q2, lambda_k2,
                               attn_subln_g, conv_w, conv_b, a_log_fwd, a_log_bwd,
                               dt_bias_fwd, dt_bias_bwd, d_skip, ssd_norm_g, w_out,
                               mix_post_g, ffn2_pre_g, ffn2_w_gate, ffn2_w_up, ffn2_w_down,
                               ffn2_post_g, final_g)))
    bsz, seq, d = x.shape
    h = x.reshape(bsz * seq, d)
    for i in range(ffn1_pre_g.shape[0]):
        h = _layer(h, i, {n: a[i] for n, a in stacked.items()}, bsz, seq)
    return h.reshape(bsz, seq, d)
```

```python
import functools
import math

import jax
import jax.numpy as jnp
from jax import lax
from jax.experimental import pallas as pl
from jax.experimental.pallas import tpu as pltpu

F32 = jnp.float32
BF16 = jnp.bfloat16

EPS = 1e-6
ATTN_HEADS = 8
ATTN_QK_DIM = 64
ATTN_V_DIM = 2 * ATTN_QK_DIM
ROT_DIM = ATTN_QK_DIM // 4
ROPE_THETA = 500000.0
SSD_HEADS = 16
SSD_HEAD_DIM = 64
SSD_GROUPS = 2
SSD_STATE = 128
SSD_CONV = 5
SSD_CHUNK = 128
SSD_CHUNKS_PER_STEP = 8

LANES = 128
MXU_COLS = 256
BF16_ROWS = 16
LOG2_E = math.log2(math.e)
ATTN_QB = MXU_COLS
ATTN_KB = MXU_COLS
ATTN_STEPS_PER_TRIP = 16
ROW_TILE = 512
FFN_ROW_GROUPS = 4
CONV_ROWS = 64
SUBLANES = 8
VMEM_LIMIT_BYTES = 56 * 1024 * 1024


def _params(*semantics):
    return pltpu.CompilerParams(dimension_semantics=semantics,
                                vmem_limit_bytes=VMEM_LIMIT_BYTES)


def _resident(shape):
    zeros = (0,) * len(shape)
    return pl.BlockSpec(shape, lambda *_: zeros, pipeline_mode=pl.Buffered(1))


def _rms(x, g):
    return x * lax.rsqrt(jnp.mean(x * x, axis=-1, keepdims=True) + EPS) * g


def _silu(x):
    return x * jax.nn.sigmoid(x)


def _dot(a, b):
    return jnp.dot(a, b, preferred_element_type=F32)


def _dot_nt(a, b):
    return lax.dot_general(a, b, (((1,), (1,)), ((), ())), preferred_element_type=F32)


def _swiglu_half_step(x, gpre, wg_ref, wu_ref, wd_ref, gpost):
    outs = []
    rows = x.shape[0] // FFN_ROW_GROUPS
    xn = _rms(x, gpre).astype(BF16)
    acts = []
    for r in range(FFN_ROW_GROUPS):
        xr = xn[r * rows:(r + 1) * rows]
        acts.append((_silu(_dot(xr, wg_ref[...])) * _dot(xr, wu_ref[...])).astype(BF16))
    for r in range(FFN_ROW_GROUPS):
        f = _dot(acts[r], wd_ref[...])
        outs.append(x[r * rows:(r + 1) * rows] + 0.5 * _rms(f, gpost))
    return jnp.concatenate(outs, axis=0)


def _ffn_kernel(x_ref, gpre_ref, wg_ref, wu_ref, wd_ref, gpost_ref, o_ref):
    o_ref[...] = _swiglu_half_step(x_ref[...], gpre_ref[...], wg_ref, wu_ref, wd_ref,
                                   gpost_ref[...])


def _ffn(x2d, gpre, wg, wu, wd, gpost, tm):
    t, d = x2d.shape
    f = wg.shape[1]
    row = pl.BlockSpec((tm, d), lambda i: (i, 0))
    return pl.pallas_call(
        _ffn_kernel,
        grid=(t // tm,),
        in_specs=[row, _resident((1, d)), _resident((d, f)), _resident((d, f)),
                  _resident((f, d)), _resident((1, d))],
        out_specs=row,
        out_shape=jax.ShapeDtypeStruct((t, d), F32),
        compiler_params=_params("parallel"),
        name="ffn1",
    )(x2d, gpre, wg, wu, wd, gpost)


def _in_proj_kernel(hp_ref, h_ref, hx_ref, g_ref, w_ref, wvt_ref, cos_ref, sa_ref, sb_ref,
                    dtb_ref, cw_ref, cb_ref, q_ref, k_ref, vt_ref, z_ref, xc_ref, dt_ref, *,
                    qk_cols, z_cols, xbc_cols, nb):
    tm = h_ref.shape[0]
    hn = _rms(h_ref[...], g_ref[...]).astype(BF16)
    hn_ext = jnp.concatenate([_rms(hp_ref[...], g_ref[...]).astype(BF16), hn,
                              _rms(hx_ref[...], g_ref[...]).astype(BF16)], axis=0)
    i = pl.program_id(0) % nb
    row = lax.broadcasted_iota(jnp.int32, (tm + 2 * SUBLANES, 1), 0)
    inside = jnp.logical_and(jnp.logical_or(i > 0, row >= SUBLANES),
                             jnp.logical_or(i < nb - 1, row < tm + SUBLANES))
    pad = SSD_CONV // 2
    c_z = 2 * qk_cols
    c_x = c_z + z_cols
    c_dt = c_x + xbc_cols
    cos = cos_ref[...]
    sa = sa_ref[...]
    sb = sb_ref[...]

    def conv_dot(c):
        ext = _dot(hn_ext, w_ref[:, c_x + c * MXU_COLS:c_x + (c + 1) * MXU_COLS])
        return jnp.where(inside, ext, 0.0)

    def conv_taps(c, ext):
        cols = slice(c * MXU_COLS, (c + 1) * MXU_COLS)
        for r0 in range(0, tm, CONV_ROWS):
            piece = ext[r0:r0 + CONV_ROWS + 2 * SUBLANES]
            acc = jnp.zeros((CONV_ROWS, MXU_COLS), F32) + cb_ref[:, cols]
            for k in range(SSD_CONV):
                off = SUBLANES - pad + k
                acc = acc + piece[off:off + CONV_ROWS] * cw_ref[k:k + 1, cols]
            xc_ref[r0:r0 + CONV_ROWS, cols] = _silu(acc).astype(xc_ref.dtype)

    def rotary_block(col0, out_ref, mult, c):
        t2 = _dot(hn, w_ref[:, col0 + c * MXU_COLS:col0 + (c + 1) * MXU_COLS])
        for half in range(MXU_COLS // LANES):
            t = t2[:, half * LANES:(half + 1) * LANES]
            r = t * cos + pltpu.roll(t, LANES - ROT_DIM // 2, 1) * sa \
                + pltpu.roll(t, ROT_DIM // 2, 1) * sb
            out_ref[0, c * (MXU_COLS // LANES) + half] = (r * mult).astype(out_ref.dtype)

    def z_block(c):
        cols = slice(c * MXU_COLS, (c + 1) * MXU_COLS)
        z_ref[:, cols] = _dot(hn, w_ref[:, c_z + c * MXU_COLS:c_z + (c + 1) * MXU_COLS]
                              ).astype(z_ref.dtype)

    n_conv = xbc_cols // MXU_COLS
    n_rot = qk_cols // MXU_COLS
    units = ([functools.partial(rotary_block, 0, q_ref, ATTN_QK_DIM ** -0.5 * LOG2_E, c)
              for c in range(n_rot)]
             + [functools.partial(rotary_block, qk_cols, k_ref, 1.0, c) for c in range(n_rot)]
             + [functools.partial(z_block, c) for c in range(z_cols // MXU_COLS)])
    per_conv = -(-len(units) // n_conv)
    for c in range(n_conv):
        ext = conv_dot(c)
        for unit in units[c * per_conv:(c + 1) * per_conv]:
            unit()
        conv_taps(c, ext)
    vt_ref[0] = _dot_nt(wvt_ref[...], hn).astype(vt_ref.dtype).reshape(vt_ref.shape[1:])
    dt_raw = _dot(hn, w_ref[:, c_dt:c_dt + LANES]) + dtb_ref[...]
    dt_ref[...] = jax.nn.softplus(dt_raw)


def _in_proj(h2d, g, w_pad, wvt, cos, sa, sb, dtb, conv_w, conv_b, tm, seq, qk_cols, z_cols,
             xbc_cols):
    t, d = h2d.shape
    per = tm // SUBLANES
    n_halo = t // SUBLANES
    prev = pl.BlockSpec((SUBLANES, d), lambda i: (jnp.maximum(i * per - 1, 0), 0))
    nxt = pl.BlockSpec((SUBLANES, d), lambda i: (jnp.minimum((i + 1) * per, n_halo - 1), 0))
    v_cols = wvt.shape[0]
    nb = seq // tm
    row = lambda n: pl.BlockSpec((tm, n), lambda i: (i, 0))
    pos = pl.BlockSpec((tm, LANES), lambda i: (i % nb, 0))
    heads = qk_cols // LANES
    qk_spec = pl.BlockSpec((1, heads, tm, LANES), lambda i: (i // nb, 0, i % nb, 0))
    vt_spec = pl.BlockSpec((1, heads, v_cols // heads, tm), lambda i: (i // nb, 0, 0, i % nb))
    kern = functools.partial(_in_proj_kernel, qk_cols=qk_cols, z_cols=z_cols,
                             xbc_cols=xbc_cols, nb=nb)
    return pl.pallas_call(
        kern,
        grid=(t // tm,),
        in_specs=[prev, row(d), nxt, _resident((1, d)), _resident(w_pad.shape),
                  _resident(wvt.shape), pos, pos, pos, _resident((1, LANES)),
                  _resident(conv_w.shape), _resident(conv_b.shape)],
        out_specs=[qk_spec, qk_spec, vt_spec, row(z_cols), row(xbc_cols), row(LANES)],
        out_shape=[jax.ShapeDtypeStruct((t // seq, heads, seq, LANES), BF16),
                   jax.ShapeDtypeStruct((t // seq, heads, seq, LANES), BF16),
                   jax.ShapeDtypeStruct((t // seq, heads, v_cols // heads, seq), BF16),
                   jax.ShapeDtypeStruct((t, z_cols), BF16),
                   jax.ShapeDtypeStruct((t, xbc_cols), BF16),
                   jax.ShapeDtypeStruct((t, LANES), F32)],
        compiler_params=_params("parallel"),
        name="in_proj",
    )(h2d, h2d, h2d, g, w_pad, wvt, cos, sa, sb, dtb, conv_w, conv_b)


def _attn_kernel(q_ref, k_ref, vt_ref, lam_ref, g_ref, o_ref, st_a_ref, st_b_ref, *, lam_init):
    heads, seq = k_ref.shape[1], k_ref.shape[2]
    n_q = seq // ATTN_QB
    n_kb = seq // ATTN_KB
    n_steps = heads * n_q
    lp = lam_ref[...]
    lam = (jnp.exp(jnp.sum(lp[0:1] * lp[1:2], axis=-1, keepdims=True))
           - jnp.exp(jnp.sum(lp[2:3] * lp[3:4], axis=-1, keepdims=True)) + lam_init)
    first = lax.broadcasted_iota(jnp.int32, (ATTN_V_DIM, ATTN_QB), 0) < ATTN_QK_DIM
    ones_rows = jnp.ones((BF16_ROWS, ATTN_KB), BF16)

    def query_rows(c):
        return pl.ds(pl.multiple_of((c % n_q) * ATTN_QB, ATTN_QB), ATTN_QB)

    def both_maps(c):
        qt = q_ref[0, c // n_q, query_rows(c), :].astype(F32).T.astype(BF16)
        zero = jnp.zeros_like(qt)
        return jnp.concatenate([jnp.where(first, qt, zero), jnp.where(first, zero, qt)], axis=1)

    def scores_block(c, q2, st_ref, kb, m8):
        keys = slice(kb * ATTN_KB, (kb + 1) * ATTN_KB)
        st = _dot(k_ref[0, c // n_q, keys, :], q2)
        st_ref[keys, :] = st
        return jnp.maximum(m8, jnp.max(st.reshape(ATTN_KB // 8, 8, 2 * ATTN_QB), axis=0))

    def values_block(c, st_ref, kb, m, acc):
        keys = slice(kb * ATTN_KB, (kb + 1) * ATTN_KB)
        p = jnp.exp2(st_ref[keys, :] - m).astype(BF16)
        vt_aug = jnp.concatenate([vt_ref[0, c // n_q, :, keys], ones_rows], axis=0)
        return acc + _dot(vt_aug, p)

    def finish(c, acc):
        o1 = acc[:ATTN_V_DIM, :ATTN_QB] * (1.0 / acc[ATTN_V_DIM:ATTN_V_DIM + 1, :ATTN_QB])
        o2 = acc[:ATTN_V_DIM, ATTN_QB:] * (1.0 / acc[ATTN_V_DIM:ATTN_V_DIM + 1, ATTN_QB:])
        o = (o1 - lam * o2).T
        o_ref[0, c // n_q, query_rows(c), :] = (
            _rms(o, g_ref[...]) * (1.0 - lam_init)).astype(o_ref.dtype)

    m8_init = jnp.full((8, 2 * ATTN_QB), -jnp.inf, F32)
    acc_init = jnp.zeros((ATTN_V_DIM + BF16_ROWS, 2 * ATTN_QB), F32)
    buffers = (st_a_ref, st_b_ref)

    def scores_only(c):
        q2 = both_maps(c)
        m8 = m8_init
        for kb in range(n_kb):
            m8 = scores_block(c, q2, buffers[0], kb, m8)
        return jnp.max(m8, axis=0, keepdims=True)

    def values_only(c, parity, m):
        acc = acc_init
        for kb in range(n_kb):
            acc = values_block(c, buffers[parity], kb, m, acc)
        finish(c, acc)

    def overlapped(c, parity, m_prev):
        q2 = both_maps(c)
        m8, acc = m8_init, acc_init
        for kb in range(n_kb):
            m8 = scores_block(c, q2, buffers[parity], kb, m8)
            acc = values_block(c - 1, buffers[1 - parity], kb, m_prev, acc)
        finish(c - 1, acc)
        return jnp.max(m8, axis=0, keepdims=True)

    m = scores_only(0)
    n_trips = (n_steps - 1) // ATTN_STEPS_PER_TRIP

    def trip(i, m):
        for j in range(ATTN_STEPS_PER_TRIP):
            m = overlapped(i * ATTN_STEPS_PER_TRIP + 1 + j, (1 + j) % 2, m)
        return m

    m = lax.fori_loop(0, n_trips, trip, m)
    for c in range(n_trips * ATTN_STEPS_PER_TRIP + 1, n_steps):
        m = overlapped(c, c % 2, m)
    values_only(n_steps - 1, (n_steps - 1) % 2, m)


def _attention(q, k, vt, lam_params, g, lam_init):
    bsz, heads, seq, _ = q.shape
    qo = pl.BlockSpec((1, heads, seq, ATTN_V_DIM), lambda b: (b, 0, 0, 0))
    vt_spec = pl.BlockSpec((1, heads, ATTN_V_DIM, seq), lambda b: (b, 0, 0, 0))
    kern = functools.partial(_attn_kernel, lam_init=lam_init)
    return pl.pallas_call(
        kern,
        grid=(bsz,),
        in_specs=[qo, qo, vt_spec, _resident(lam_params.shape), _resident(g.shape)],
        out_specs=qo,
        out_shape=jax.ShapeDtypeStruct(q.shape, BF16),
        scratch_shapes=[pltpu.VMEM((seq, 2 * ATTN_QB), F32), pltpu.VMEM((seq, 2 * ATTN_QB), F32)],
        compiler_params=_params("parallel"),
        name="attention",
    )(q, k, vt, lam_params, g)


def _cumsum_rows(x, reverse):
    n = x.shape[0]
    row = lax.broadcasted_iota(jnp.int32, x.shape, 0)
    step = 1
    while step < n:
        if reverse:
            shifted = pltpu.roll(x, n - step, 0)
            x = x + jnp.where(row < n - step, shifted, 0.0)
        else:
            shifted = pltpu.roll(x, step, 0)
            x = x + jnp.where(row >= step, shifted, 0.0)
        step *= 2
    return x


def _ssd_direction(xc_ref, dt_ref, alog_ref, state_ref, y_ref, tiles_ref, cb_ref, bt_ref, *,
                   lane0, reverse, row0):
    L = SSD_CHUNK
    n_x = SSD_HEADS * SSD_HEAD_DIM
    gw = SSD_STATE
    rows = slice(row0, row0 + L)
    dt = dt_ref[0, rows, :]
    a = -jnp.exp(alog_ref[...]) * LOG2_E
    cs = _cumsum_rows(dt * a, reverse)
    tot = cs[0:1] if reverse else cs[L - 1:L]
    w = dt * jnp.exp2(tot - cs)
    log_dt = jnp.log2(dt)
    tiles_ref[0] = cs
    tiles_ref[1] = (cs - log_dt).T
    tiles_ref[2] = log_dt.T
    tiles_ref[3] = w.T
    row = lax.broadcasted_iota(jnp.int32, (L, L), 0)
    col = lax.broadcasted_iota(jnp.int32, (L, L), 1)
    keep = (row <= col) if reverse else (row >= col)
    first = lax.broadcasted_iota(jnp.int32, (L, LANES), 1) < SSD_HEAD_DIM

    def group_cols(kind, g):
        lo = n_x + (kind * SSD_GROUPS + g) * gw
        return slice(lo, lo + gw)

    for g in range(SSD_GROUPS):
        b_g = xc_ref[0, rows, group_cols(0, g)]
        cb_ref[g] = jnp.where(keep, _dot_nt(xc_ref[0, rows, group_cols(1, g)], b_g), 0.0
                              ).astype(BF16)
        bt_ref[g] = b_g.astype(F32).T.astype(BF16)

    heads_per_group = SSD_HEADS // SSD_GROUPS
    for pair in range(SSD_HEADS // 2):
        cols = slice(pair * LANES, (pair + 1) * LANES)
        xs = xc_ref[0, rows, cols]
        prev = state_ref[:, cols]
        rhs = jnp.concatenate([xs, prev.astype(BF16)], axis=0)
        ys, ss, cds = [], [], []
        for h in (2 * pair, 2 * pair + 1):
            g = h // heads_per_group
            hl = lane0 + h
            col_cs = jnp.broadcast_to(tiles_ref[0, :, hl:hl + 1], (L, L))
            row_cs = jnp.broadcast_to(tiles_ref[1, hl:hl + 1, :], (L, L))
            row_ldt = jnp.broadcast_to(tiles_ref[2, hl:hl + 1, :], (L, L))
            decay_dt = jnp.exp2(jnp.minimum(col_cs - row_cs, row_ldt))
            m = cb_ref[g] * decay_dt.astype(BF16)
            c_e = xc_ref[0, rows, group_cols(1, g)] * jnp.exp2(col_cs).astype(BF16)
            ys.append(_dot(jnp.concatenate([m, c_e], axis=1), rhs))
            b_w = bt_ref[g] * jnp.broadcast_to(tiles_ref[3, hl:hl + 1, :], (gw, L)).astype(BF16)
            ss.append(_dot(b_w, xs))
            cds.append(jnp.broadcast_to(jnp.exp2(tot[:, hl:hl + 1]), (1, LANES)))
        y_ref[0, rows, cols] = jnp.where(first, ys[0], ys[1]).astype(y_ref.dtype)
        chunk_decay = jnp.where(first[0:1], cds[0], cds[1])
        state_ref[:, cols] = prev * chunk_decay + jnp.where(first, ss[0], ss[1])


def _ssd_kernel(xf_ref, xb_ref, dtf_ref, dtb_ref, alog_ref, yf_ref, yb_ref, sf_ref, sb_ref,
                tiles_f_ref, tiles_b_ref, cb_f_ref, cb_b_ref, bt_f_ref, bt_b_ref):
    @pl.when(pl.program_id(1) == 0)
    def _():
        sf_ref[...] = jnp.zeros_like(sf_ref)
        sb_ref[...] = jnp.zeros_like(sb_ref)

    n_sub = xf_ref.shape[1] // SSD_CHUNK
    for j in range(n_sub):
        _ssd_direction(xf_ref, dtf_ref, alog_ref, sf_ref, yf_ref, tiles_f_ref, cb_f_ref, bt_f_ref,
                       lane0=0, reverse=False, row0=j * SSD_CHUNK)
        _ssd_direction(xb_ref, dtb_ref, alog_ref, sb_ref, yb_ref, tiles_b_ref, cb_b_ref, bt_b_ref,
                       lane0=SSD_HEADS, reverse=True, row0=(n_sub - 1 - j) * SSD_CHUNK)


def _ssd(xconv, dt, alog):
    bsz, seq, n_ch = xconv.shape
    n_x = SSD_HEADS * SSD_HEAD_DIM
    rows = SSD_CHUNK * min(SSD_CHUNKS_PER_STEP, seq // SSD_CHUNK)
    nc = seq // rows
    fwd = lambda n: pl.BlockSpec((1, rows, n), lambda b, t: (b, t, 0))
    bwd = lambda n: pl.BlockSpec((1, rows, n), lambda b, t: (b, nc - 1 - t, 0))
    y_shape = jax.ShapeDtypeStruct((bsz, seq, n_x), BF16)
    return pl.pallas_call(
        _ssd_kernel,
        grid=(bsz, nc),
        in_specs=[fwd(n_ch), bwd(n_ch), fwd(LANES), bwd(LANES), _resident(alog.shape)],
        out_specs=[fwd(n_x), bwd(n_x)],
        out_shape=[y_shape, y_shape],
        scratch_shapes=[pltpu.VMEM((SSD_STATE, n_x), F32), pltpu.VMEM((SSD_STATE, n_x), F32),
                        pltpu.VMEM((4, SSD_CHUNK, LANES), F32), pltpu.VMEM((4, SSD_CHUNK, LANES), F32),
                        pltpu.VMEM((SSD_GROUPS, SSD_CHUNK, SSD_CHUNK), BF16),
                        pltpu.VMEM((SSD_GROUPS, SSD_CHUNK, SSD_CHUNK), BF16),
                        pltpu.VMEM((SSD_GROUPS, SSD_STATE, SSD_CHUNK), BF16),
                        pltpu.VMEM((SSD_GROUPS, SSD_STATE, SSD_CHUNK), BF16)],
        compiler_params=_params("parallel", "arbitrary"),
        name="ssd",
    )(xconv, xconv, dt, dt, alog)


def _out_ffn_kernel(h_ref, attn_ref, yf_ref, yb_ref, xs_ref, z_ref, dskip_ref, gssd_ref,
                    wo_ref, gmix_ref, gpre_ref, wg_ref, wu_ref, wd_ref, gpost_ref,
                    gfinal_ref, o_ref):
    heads = attn_ref.shape[1]
    attn = jnp.concatenate([attn_ref[0, h] for h in range(heads)], axis=1)
    n_attn = attn.shape[-1]
    y = (yf_ref[...].astype(F32) + yb_ref[...].astype(F32)
         + xs_ref[...].astype(F32) * dskip_ref[...])
    y = _rms(y * _silu(z_ref[...].astype(F32)), gssd_ref[...])
    m = _dot(attn, wo_ref[:n_attn, :]) + _dot(y.astype(BF16), wo_ref[n_attn:, :])
    h = h_ref[...] + _rms(m, gmix_ref[...])
    h = _swiglu_half_step(h, gpre_ref[...], wg_ref, wu_ref, wd_ref, gpost_ref[...])
    o_ref[...] = _rms(h, gfinal_ref[...])


def _out_ffn(h2d, attn, yf, yb, xconv, z, dskip, gssd, wo, gmix, gpre, wg, wu, wd, gpost,
             gfinal, tm):
    t, d = h2d.shape
    _, heads, seq, v_dim = attn.shape
    nb = seq // tm
    attn_spec = pl.BlockSpec((1, heads, tm, v_dim), lambda i: (i // nb, 0, i % nb, 0))
    f = wg.shape[1]
    n_x = yf.shape[-1]
    row = lambda n: pl.BlockSpec((tm, n), lambda i: (i, 0))
    return pl.pallas_call(
        _out_ffn_kernel,
        grid=(t // tm,),
        in_specs=[row(d), attn_spec, row(n_x), row(n_x), row(n_x), row(n_x),
                  _resident((1, n_x)), _resident((1, n_x)), _resident(wo.shape),
                  _resident((1, d)), _resident((1, d)), _resident((d, f)),
                  _resident((d, f)), _resident((f, d)), _resident((1, d)),
                  _resident((1, d))],
        out_specs=row(d),
        out_shape=jax.ShapeDtypeStruct((t, d), F32),
        compiler_params=_params("parallel"),
        name="out_ffn",
    )(h2d, attn, yf, yb, xconv, z, dskip, gssd, wo, gmix, gpre, wg, wu, wd, gpost, gfinal)


def _rotary_tables(seq):
    half = ROT_DIM // 2
    pos = jnp.arange(seq, dtype=F32)
    inv_freq = jnp.power(F32(ROPE_THETA), -jnp.arange(0, ROT_DIM, 2, dtype=F32) / ROT_DIM)
    ang = pos[:, None] * inv_freq[None, :]
    cos, sin = jnp.cos(ang), jnp.sin(ang)
    ones = jnp.ones((seq, ATTN_QK_DIM - ROT_DIM), F32)
    zeros_h = jnp.zeros((seq, half), F32)
    zeros_r = jnp.zeros((seq, ATTN_QK_DIM - ROT_DIM), F32)
    reps = LANES // ATTN_QK_DIM
    cos_t = jnp.tile(jnp.concatenate([cos, cos, ones], axis=1), (1, reps))
    sa_t = jnp.tile(jnp.concatenate([-sin, zeros_h, zeros_r], axis=1), (1, reps))
    sb_t = jnp.tile(jnp.concatenate([zeros_h, sin, zeros_r], axis=1), (1, reps))
    return cos_t, sa_t, sb_t


def _pad_lanes(v, width=LANES):
    return jnp.pad(v, ((0, 0), (0, width - v.shape[-1])))


def _layer(h2d, layer_idx, p, bsz, seq):
    t, d = h2d.shape
    tm = min(ROW_TILE, seq)
    row = lambda v: v.reshape(1, -1).astype(F32)
    bf = lambda w: w.astype(BF16)

    h1 = _ffn(h2d, row(p["ffn1_pre_g"]), bf(p["ffn1_w_gate"]), bf(p["ffn1_w_up"]),
              bf(p["ffn1_w_down"]), row(p["ffn1_post_g"]), tm)

    qk_cols = ATTN_HEADS * 2 * ATTN_QK_DIM
    v_cols = ATTN_HEADS * ATTN_V_DIM
    n_x = SSD_HEADS * SSD_HEAD_DIM
    xbc_cols = n_x + 2 * SSD_GROUPS * SSD_STATE
    w_in = p["w_in"]
    v0 = 2 * qk_cols
    wvt = bf(w_in[:, v0:v0 + v_cols].T)
    n_dt = w_in.shape[1] - (v0 + v_cols + n_x + xbc_cols)
    w_pad = bf(jnp.pad(jnp.concatenate([w_in[:, :v0], w_in[:, v0 + v_cols:]], axis=1),
                       ((0, 0), (0, LANES - n_dt))))
    dt_bias = _pad_lanes(jnp.concatenate([p["dt_bias_fwd"], p["dt_bias_bwd"]]).reshape(1, -1)
                         .astype(F32))
    alog = _pad_lanes(jnp.concatenate([p["a_log_fwd"], p["a_log_bwd"]]).reshape(1, -1)
                      .astype(F32))
    cos_t, sa_t, sb_t = _rotary_tables(seq)
    q, k, vt, z, xconv, dt = _in_proj(h1, row(p["mix_pre_g"]), w_pad, wvt, cos_t, sa_t, sb_t,
                                      dt_bias, p["conv_w"].astype(F32), row(p["conv_b"]), tm, seq,
                                      qk_cols, n_x, xbc_cols)
    shape3 = lambda a: a.reshape(bsz, seq, a.shape[-1])
    xconv = shape3(xconv)

    lam_init = 0.8 - 0.6 * math.exp(-0.3 * layer_idx)
    lam_params = jnp.stack([p["lambda_q1"], p["lambda_k1"], p["lambda_q2"],
                            p["lambda_k2"]]).astype(F32)
    attn = _attention(q, k, vt, lam_params, row(p["attn_subln_g"]), lam_init)

    yf, yb = _ssd(xconv, shape3(dt), alog)

    dskip = jnp.repeat(p["d_skip"].astype(F32), SSD_HEAD_DIM).reshape(1, -1)
    flat = lambda a: a.reshape(t, a.shape[-1])
    return _out_ffn(h1, attn, flat(yf), flat(yb), flat(xconv), z, dskip,
                    row(p["ssd_norm_g"]), bf(p["w_out"]), row(p["mix_post_g"]),
                    row(p["ffn2_pre_g"]), bf(p["ffn2_w_gate"]), bf(p["ffn2_w_up"]),
                    bf(p["ffn2_w_down"]), row(p["ffn2_post_g"]), row(p["final_g"]), tm)


def kernel(x, ffn1_pre_g, ffn1_w_gate, ffn1_w_up, ffn1_w_down, ffn1_post_g, mix_pre_g, w_in, lambda_q1, lambda_k1, lambda_q2, lambda_k2, attn_subln_g, conv_w, conv_b, a_log_fwd, a_log_bwd, dt_bias_fwd, dt_bias_bwd, d_skip, ssd_norm_g, w_out, mix_post_g, ffn2_pre_g, ffn2_w_gate, ffn2_w_up, ffn2_w_down, ffn2_post_g, final_g):
    names = ("ffn1_pre_g", "ffn1_w_gate", "ffn1_w_up", "ffn1_w_down", "ffn1_post_g",
             "mix_pre_g", "w_in", "lambda_q1", "lambda_k1", "lambda_q2", "lambda_k2",
             "attn_subln_g", "conv_w", "conv_b", "a_log_fwd", "a_log_bwd", "dt_bias_fwd",
             "dt_bias_bwd", "d_skip", "ssd_norm_g", "w_out", "mix_post_g", "ffn2_pre_g",
             "ffn2_w_gate", "ffn2_w_up", "ffn2_w_down", "ffn2_post_g", "final_g")
    stacked = dict(zip(names, (ffn1_pre_g, ffn1_w_gate, ffn1_w_up, ffn1_w_down, ffn1_post_g,
                               mix_pre_g, w_in, lambda_q1, lambda_k1, lambda_q2, lambda_k2,
                               attn_subln_g, conv_w, conv_b, a_log_fwd, a_log_bwd,
                               dt_bias_fwd, dt_bias_bwd, d_skip, ssd_norm_g, w_out,
                               mix_post_g, ffn2_pre_g, ffn2_w_gate, ffn2_w_up, ffn2_w_down,
                               ffn2_post_g, final_g)))
    bsz, seq, d = x.shape
    h = x.reshape(bsz * seq, d)
    for i in range(ffn1_pre_g.shape[0]):
        h = _layer(h, i, {n: a[i] for n, a in stacked.items()}, bsz, seq)
    return h.reshape(bsz, seq, d)
```

```python
import functools
import itertools
import math

import jax
import jax.numpy as jnp
from jax import lax
from jax.experimental import pallas as pl
from jax.experimental.pallas import tpu as pltpu

F32 = jnp.float32
BF16 = jnp.bfloat16

EPS = 1e-6
ATTN_HEADS = 8
ATTN_QK_DIM = 64
ATTN_V_DIM = 2 * ATTN_QK_DIM
ROT_DIM = ATTN_QK_DIM // 4
ROPE_THETA = 500000.0
SSD_HEADS = 16
SSD_HEAD_DIM = 64
SSD_GROUPS = 2
SSD_STATE = 128
SSD_CONV = 5
SSD_CHUNK = 128
SSD_CHUNKS_PER_STEP = 8

LANES = 128
MXU_COLS = 256
BF16_ROWS = 16
LOG2_E = math.log2(math.e)
ATTN_QB = MXU_COLS
ATTN_KB = MXU_COLS
ATTN_STEPS_PER_TRIP = 16
ROW_TILE = 512
FFN_ROW_GROUPS = 2
OUT_ROWS = 64
CONV_ROWS = 64
SUBLANES = 8
VMEM_LIMIT_BYTES = 56 * 1024 * 1024


def _params(*semantics):
    return pltpu.CompilerParams(dimension_semantics=semantics,
                                vmem_limit_bytes=VMEM_LIMIT_BYTES)


def _resident(shape):
    zeros = (0,) * len(shape)
    return pl.BlockSpec(shape, lambda *_: zeros, pipeline_mode=pl.Buffered(1))


def _rms(x, g):
    return x * lax.rsqrt(jnp.mean(x * x, axis=-1, keepdims=True) + EPS) * g


def _silu(x):
    return x * jax.nn.sigmoid(x)


def _dot(a, b):
    return jnp.dot(a, b, preferred_element_type=F32)


def _dot_nt(a, b):
    return lax.dot_general(a, b, (((1,), (1,)), ((), ())), preferred_element_type=F32)


def _swiglu_half_step(x, gpre, wg_ref, wu_ref, wd_ref, gpost):
    outs = []
    rows = x.shape[0] // FFN_ROW_GROUPS
    xn = _rms(x, gpre).astype(BF16)
    acts = []
    for r in range(FFN_ROW_GROUPS):
        xr = xn[r * rows:(r + 1) * rows]
        acts.append((_silu(_dot(xr, wg_ref[...])) * _dot(xr, wu_ref[...])).astype(BF16))
    for r in range(FFN_ROW_GROUPS):
        f = _dot(acts[r], wd_ref[...])
        outs.append(x[r * rows:(r + 1) * rows] + 0.5 * _rms(f, gpost))
    return jnp.concatenate(outs, axis=0)


def _ffn_kernel(x_ref, gpre_ref, wg_ref, wu_ref, wd_ref, gpost_ref, o_ref):
    o_ref[...] = _swiglu_half_step(x_ref[...], gpre_ref[...], wg_ref, wu_ref, wd_ref,
                                   gpost_ref[...])


def _ffn(x2d, gpre, wg, wu, wd, gpost, tm):
    t, d = x2d.shape
    f = wg.shape[1]
    row = pl.BlockSpec((tm, d), lambda i: (i, 0))
    return pl.pallas_call(
        _ffn_kernel,
        grid=(t // tm,),
        in_specs=[row, _resident((1, d)), _resident((d, f)), _resident((d, f)),
                  _resident((f, d)), _resident((1, d))],
        out_specs=row,
        out_shape=jax.ShapeDtypeStruct((t, d), F32),
        compiler_params=_params("parallel"),
        name="ffn1",
    )(x2d, gpre, wg, wu, wd, gpost)


def _in_proj_kernel(hp_ref, h_ref, hx_ref, g_ref, w_ref, wvt_ref, cos_ref, sa_ref, sb_ref,
                    dtb_ref, cw_ref, cb_ref, q_ref, k_ref, vt_ref, z_ref, xc_ref, dt_ref, *,
                    qk_cols, z_cols, xbc_cols, nb):
    tm = h_ref.shape[0]
    hn = _rms(h_ref[...], g_ref[...]).astype(BF16)
    hn_ext = jnp.concatenate([_rms(hp_ref[...], g_ref[...]).astype(BF16), hn,
                              _rms(hx_ref[...], g_ref[...]).astype(BF16)], axis=0)
    i = pl.program_id(0) % nb
    row = lax.broadcasted_iota(jnp.int32, (tm + 2 * SUBLANES, 1), 0)
    inside = jnp.logical_and(jnp.logical_or(i > 0, row >= SUBLANES),
                             jnp.logical_or(i < nb - 1, row < tm + SUBLANES))
    pad = SSD_CONV // 2
    c_z = 2 * qk_cols
    c_x = c_z + z_cols
    c_dt = c_x + xbc_cols
    cos = cos_ref[...]
    sa = sa_ref[...]
    sb = sb_ref[...]

    def conv_dot(c):
        ext = _dot(hn_ext, w_ref[:, c_x + c * MXU_COLS:c_x + (c + 1) * MXU_COLS])
        return jnp.where(inside, ext, 0.0)

    def conv_taps(c, ext):
        cols = slice(c * MXU_COLS, (c + 1) * MXU_COLS)
        for r0 in range(0, tm, CONV_ROWS):
            piece = ext[r0:r0 + CONV_ROWS + 2 * SUBLANES]
            acc = jnp.zeros((CONV_ROWS, MXU_COLS), F32) + cb_ref[:, cols]
            for k in range(SSD_CONV):
                off = SUBLANES - pad + k
                acc = acc + piece[off:off + CONV_ROWS] * cw_ref[k:k + 1, cols]
            xc_ref[r0:r0 + CONV_ROWS, cols] = _silu(acc).astype(xc_ref.dtype)

    def rotary_block(col0, out_ref, mult, c):
        t2 = _dot(hn, w_ref[:, col0 + c * MXU_COLS:col0 + (c + 1) * MXU_COLS])
        for half in range(MXU_COLS // LANES):
            t = t2[:, half * LANES:(half + 1) * LANES]
            r = t * cos + pltpu.roll(t, LANES - ROT_DIM // 2, 1) * sa \
                + pltpu.roll(t, ROT_DIM // 2, 1) * sb
            out_ref[0, c * (MXU_COLS // LANES) + half] = (r * mult).astype(out_ref.dtype)

    def z_block(c):
        cols = slice(c * MXU_COLS, (c + 1) * MXU_COLS)
        z_ref[:, cols] = _dot(hn, w_ref[:, c_z + c * MXU_COLS:c_z + (c + 1) * MXU_COLS]
                              ).astype(z_ref.dtype)

    n_conv = xbc_cols // MXU_COLS
    n_rot = qk_cols // MXU_COLS
    units = ([functools.partial(rotary_block, 0, q_ref, ATTN_QK_DIM ** -0.5 * LOG2_E, c)
              for c in range(n_rot)]
             + [functools.partial(rotary_block, qk_cols, k_ref, 1.0, c) for c in range(n_rot)]
             + [functools.partial(z_block, c) for c in range(z_cols // MXU_COLS)])
    per_conv = -(-len(units) // n_conv)
    for c in range(n_conv):
        ext = conv_dot(c)
        for unit in units[c * per_conv:(c + 1) * per_conv]:
            unit()
        conv_taps(c, ext)
    vt_ref[0] = _dot_nt(wvt_ref[...], hn).astype(vt_ref.dtype).reshape(vt_ref.shape[1:])
    dt_raw = _dot(hn, w_ref[:, c_dt:c_dt + LANES]) + dtb_ref[...]
    dt_ref[...] = jax.nn.softplus(dt_raw)


def _in_proj(h2d, g, w_pad, wvt, cos, sa, sb, dtb, conv_w, conv_b, tm, seq, qk_cols, z_cols,
             xbc_cols):
    t, d = h2d.shape
    per = tm // SUBLANES
    n_halo = t // SUBLANES
    prev = pl.BlockSpec((SUBLANES, d), lambda i: (jnp.maximum(i * per - 1, 0), 0))
    nxt = pl.BlockSpec((SUBLANES, d), lambda i: (jnp.minimum((i + 1) * per, n_halo - 1), 0))
    v_cols = wvt.shape[0]
    nb = seq // tm
    row = lambda n: pl.BlockSpec((tm, n), lambda i: (i, 0))
    pos = pl.BlockSpec((tm, LANES), lambda i: (i % nb, 0))
    heads = qk_cols // LANES
    qk_spec = pl.BlockSpec((1, heads, tm, LANES), lambda i: (i // nb, 0, i % nb, 0))
    vt_spec = pl.BlockSpec((1, heads, v_cols // heads, tm), lambda i: (i // nb, 0, 0, i % nb))
    kern = functools.partial(_in_proj_kernel, qk_cols=qk_cols, z_cols=z_cols,
                             xbc_cols=xbc_cols, nb=nb)
    return pl.pallas_call(
        kern,
        grid=(t // tm,),
        in_specs=[prev, row(d), nxt, _resident((1, d)), _resident(w_pad.shape),
                  _resident(wvt.shape), pos, pos, pos, _resident((1, LANES)),
                  _resident(conv_w.shape), _resident(conv_b.shape)],
        out_specs=[qk_spec, qk_spec, vt_spec, row(z_cols), row(xbc_cols), row(LANES)],
        out_shape=[jax.ShapeDtypeStruct((t // seq, heads, seq, LANES), BF16),
                   jax.ShapeDtypeStruct((t // seq, heads, seq, LANES), BF16),
                   jax.ShapeDtypeStruct((t // seq, heads, v_cols // heads, seq), BF16),
                   jax.ShapeDtypeStruct((t, z_cols), BF16),
                   jax.ShapeDtypeStruct((t, xbc_cols), BF16),
                   jax.ShapeDtypeStruct((t, LANES), F32)],
        compiler_params=_params("parallel"),
        name="in_proj",
    )(h2d, h2d, h2d, g, w_pad, wvt, cos, sa, sb, dtb, conv_w, conv_b)


def _attn_kernel(q_ref, k_ref, vt_ref, lam_ref, g_ref, o_ref, st_a_ref, st_b_ref, *, lam_init):
    heads, seq = k_ref.shape[1], k_ref.shape[2]
    n_q = seq // ATTN_QB
    n_kb = seq // ATTN_KB
    n_steps = heads * n_q
    lp = lam_ref[...]
    lam = (jnp.exp(jnp.sum(lp[0:1] * lp[1:2], axis=-1, keepdims=True))
           - jnp.exp(jnp.sum(lp[2:3] * lp[3:4], axis=-1, keepdims=True)) + lam_init)
    first = lax.broadcasted_iota(jnp.int32, (ATTN_V_DIM, ATTN_QB), 0) < ATTN_QK_DIM
    ones_rows = jnp.ones((BF16_ROWS, ATTN_KB), BF16)

    def query_rows(c):
        return pl.ds(pl.multiple_of((c % n_q) * ATTN_QB, ATTN_QB), ATTN_QB)

    def both_maps(c):
        qt = q_ref[0, c // n_q, query_rows(c), :].astype(F32).T.astype(BF16)
        zero = jnp.zeros_like(qt)
        return jnp.concatenate([jnp.where(first, qt, zero), jnp.where(first, zero, qt)], axis=1)

    def scores_block(c, q2, st_ref, kb, m8):
        keys = slice(kb * ATTN_KB, (kb + 1) * ATTN_KB)
        st = _dot(k_ref[0, c // n_q, keys, :], q2)
        st_ref[keys, :] = st
        return jnp.maximum(m8, jnp.max(st.reshape(ATTN_KB // 8, 8, 2 * ATTN_QB), axis=0))

    def values_block(c, st_ref, kb, m, acc):
        keys = slice(kb * ATTN_KB, (kb + 1) * ATTN_KB)
        p = jnp.exp2(st_ref[keys, :] - m).astype(BF16)
        vt_aug = jnp.concatenate([vt_ref[0, c // n_q, :, keys], ones_rows], axis=0)
        return acc + _dot(vt_aug, p)

    def finish(c, acc):
        o1 = acc[:ATTN_V_DIM, :ATTN_QB] * (1.0 / acc[ATTN_V_DIM:ATTN_V_DIM + 1, :ATTN_QB])
        o2 = acc[:ATTN_V_DIM, ATTN_QB:] * (1.0 / acc[ATTN_V_DIM:ATTN_V_DIM + 1, ATTN_QB:])
        o = (o1 - lam * o2).T
        o_ref[0, c // n_q, query_rows(c), :] = (
            _rms(o, g_ref[...]) * (1.0 - lam_init)).astype(o_ref.dtype)

    m8_init = jnp.full((8, 2 * ATTN_QB), -jnp.inf, F32)
    acc_init = jnp.zeros((ATTN_V_DIM + BF16_ROWS, 2 * ATTN_QB), F32)
    buffers = (st_a_ref, st_b_ref)

    def scores_only(c):
        q2 = both_maps(c)
        m8 = m8_init
        for kb in range(n_kb):
            m8 = scores_block(c, q2, buffers[0], kb, m8)
        return jnp.max(m8, axis=0, keepdims=True)

    def values_only(c, parity, m):
        acc = acc_init
        for kb in range(n_kb):
            acc = values_block(c, buffers[parity], kb, m, acc)
        finish(c, acc)

    def overlapped(c, parity, m_prev):
        q2 = both_maps(c)
        m8, acc = m8_init, acc_init
        for kb in range(n_kb):
            m8 = scores_block(c, q2, buffers[parity], kb, m8)
            acc = values_block(c - 1, buffers[1 - parity], kb, m_prev, acc)
        finish(c - 1, acc)
        return jnp.max(m8, axis=0, keepdims=True)

    m = scores_only(0)
    n_trips = (n_steps - 1) // ATTN_STEPS_PER_TRIP

    def trip(i, m):
        for j in range(ATTN_STEPS_PER_TRIP):
            m = overlapped(i * ATTN_STEPS_PER_TRIP + 1 + j, (1 + j) % 2, m)
        return m

    m = lax.fori_loop(0, n_trips, trip, m)
    for c in range(n_trips * ATTN_STEPS_PER_TRIP + 1, n_steps):
        m = overlapped(c, c % 2, m)
    values_only(n_steps - 1, (n_steps - 1) % 2, m)


def _attention(q, k, vt, lam_params, g, lam_init):
    bsz, heads, seq, _ = q.shape
    qo = pl.BlockSpec((1, heads, seq, ATTN_V_DIM), lambda b: (b, 0, 0, 0))
    vt_spec = pl.BlockSpec((1, heads, ATTN_V_DIM, seq), lambda b: (b, 0, 0, 0))
    kern = functools.partial(_attn_kernel, lam_init=lam_init)
    return pl.pallas_call(
        kern,
        grid=(bsz,),
        in_specs=[qo, qo, vt_spec, _resident(lam_params.shape), _resident(g.shape)],
        out_specs=qo,
        out_shape=jax.ShapeDtypeStruct(q.shape, BF16),
        scratch_shapes=[pltpu.VMEM((seq, 2 * ATTN_QB), F32), pltpu.VMEM((seq, 2 * ATTN_QB), F32)],
        compiler_params=_params("parallel"),
        name="attention",
    )(q, k, vt, lam_params, g)


def _cumsum_rows(x, reverse):
    n = x.shape[0]
    row = lax.broadcasted_iota(jnp.int32, x.shape, 0)
    step = 1
    while step < n:
        if reverse:
            shifted = pltpu.roll(x, n - step, 0)
            x = x + jnp.where(row < n - step, shifted, 0.0)
        else:
            shifted = pltpu.roll(x, step, 0)
            x = x + jnp.where(row >= step, shifted, 0.0)
        step *= 2
    return x


def _ssd_direction(xc_ref, dt_ref, alog_ref, state_ref, y_ref, tiles_ref, cb_ref, bt_ref, *,
                   lane0, reverse, row0):
    L = SSD_CHUNK
    n_x = SSD_HEADS * SSD_HEAD_DIM
    gw = SSD_STATE
    rows = slice(row0, row0 + L)
    dt = dt_ref[0, rows, :]
    a = -jnp.exp(alog_ref[...]) * LOG2_E
    cs = _cumsum_rows(dt * a, reverse)
    tot = cs[0:1] if reverse else cs[L - 1:L]
    w = dt * jnp.exp2(tot - cs)
    log_dt = jnp.log2(dt)
    tiles_ref[0] = cs
    tiles_ref[1] = (cs - log_dt).T
    tiles_ref[2] = log_dt.T
    tiles_ref[3] = w.T
    row = lax.broadcasted_iota(jnp.int32, (L, L), 0)
    col = lax.broadcasted_iota(jnp.int32, (L, L), 1)
    keep = (row <= col) if reverse else (row >= col)
    first = lax.broadcasted_iota(jnp.int32, (L, LANES), 1) < SSD_HEAD_DIM

    def group_cols(kind, g):
        lo = n_x + (kind * SSD_GROUPS + g) * gw
        return slice(lo, lo + gw)

    for g in range(SSD_GROUPS):
        b_g = xc_ref[0, rows, group_cols(0, g)]
        cb_ref[g] = jnp.where(keep, _dot_nt(xc_ref[0, rows, group_cols(1, g)], b_g), 0.0
                              ).astype(BF16)
        bt_ref[g] = b_g.astype(F32).T.astype(BF16)

    heads_per_group = SSD_HEADS // SSD_GROUPS
    for pair in range(SSD_HEADS // 2):
        cols = slice(pair * LANES, (pair + 1) * LANES)
        xs = xc_ref[0, rows, cols]
        prev = state_ref[:, cols]
        rhs = jnp.concatenate([xs, prev.astype(BF16)], axis=0)
        ys, ss, cds = [], [], []
        for h in (2 * pair, 2 * pair + 1):
            g = h // heads_per_group
            hl = lane0 + h
            col_cs = jnp.broadcast_to(tiles_ref[0, :, hl:hl + 1], (L, L))
            row_cs = jnp.broadcast_to(tiles_ref[1, hl:hl + 1, :], (L, L))
            row_ldt = jnp.broadcast_to(tiles_ref[2, hl:hl + 1, :], (L, L))
            decay_dt = jnp.exp2(jnp.minimum(col_cs - row_cs, row_ldt))
            m = cb_ref[g] * decay_dt.astype(BF16)
            c_e = xc_ref[0, rows, group_cols(1, g)] * jnp.exp2(col_cs).astype(BF16)
            ys.append(_dot(jnp.concatenate([m, c_e], axis=1), rhs))
            b_w = bt_ref[g] * jnp.broadcast_to(tiles_ref[3, hl:hl + 1, :], (gw, L)).astype(BF16)
            ss.append(_dot(b_w, xs))
            cds.append(jnp.broadcast_to(jnp.exp2(tot[:, hl:hl + 1]), (1, LANES)))
        y_ref[0, rows, cols] = jnp.where(first, ys[0], ys[1]).astype(y_ref.dtype)
        chunk_decay = jnp.where(first[0:1], cds[0], cds[1])
        state_ref[:, cols] = prev * chunk_decay + jnp.where(first, ss[0], ss[1])


def _ssd_kernel(xf_ref, xb_ref, dtf_ref, dtb_ref, alog_ref, yf_ref, yb_ref, sf_ref, sb_ref,
                tiles_f_ref, tiles_b_ref, cb_f_ref, cb_b_ref, bt_f_ref, bt_b_ref):
    @pl.when(pl.program_id(1) == 0)
    def _():
        sf_ref[...] = jnp.zeros_like(sf_ref)
        sb_ref[...] = jnp.zeros_like(sb_ref)

    n_sub = xf_ref.shape[1] // SSD_CHUNK
    for j in range(n_sub):
        _ssd_direction(xf_ref, dtf_ref, alog_ref, sf_ref, yf_ref, tiles_f_ref, cb_f_ref, bt_f_ref,
                       lane0=0, reverse=False, row0=j * SSD_CHUNK)
        _ssd_direction(xb_ref, dtb_ref, alog_ref, sb_ref, yb_ref, tiles_b_ref, cb_b_ref, bt_b_ref,
                       lane0=SSD_HEADS, reverse=True, row0=(n_sub - 1 - j) * SSD_CHUNK)


def _ssd(xconv, dt, alog):
    bsz, seq, n_ch = xconv.shape
    n_x = SSD_HEADS * SSD_HEAD_DIM
    rows = SSD_CHUNK * min(SSD_CHUNKS_PER_STEP, seq // SSD_CHUNK)
    nc = seq // rows
    fwd = lambda n: pl.BlockSpec((1, rows, n), lambda b, t: (b, t, 0))
    bwd = lambda n: pl.BlockSpec((1, rows, n), lambda b, t: (b, nc - 1 - t, 0))
    y_shape = jax.ShapeDtypeStruct((bsz, seq, n_x), BF16)
    return pl.pallas_call(
        _ssd_kernel,
        grid=(bsz, nc),
        in_specs=[fwd(n_ch), bwd(n_ch), fwd(LANES), bwd(LANES), _resident(alog.shape)],
        out_specs=[fwd(n_x), bwd(n_x)],
        out_shape=[y_shape, y_shape],
        scratch_shapes=[pltpu.VMEM((SSD_STATE, n_x), F32), pltpu.VMEM((SSD_STATE, n_x), F32),
                        pltpu.VMEM((4, SSD_CHUNK, LANES), F32), pltpu.VMEM((4, SSD_CHUNK, LANES), F32),
                        pltpu.VMEM((SSD_GROUPS, SSD_CHUNK, SSD_CHUNK), BF16),
                        pltpu.VMEM((SSD_GROUPS, SSD_CHUNK, SSD_CHUNK), BF16),
                        pltpu.VMEM((SSD_GROUPS, SSD_STATE, SSD_CHUNK), BF16),
                        pltpu.VMEM((SSD_GROUPS, SSD_STATE, SSD_CHUNK), BF16)],
        compiler_params=_params("parallel", "arbitrary"),
        name="ssd",
    )(xconv, xconv, dt, dt, alog)


def _interleave(*unit_lists):
    for group in itertools.zip_longest(*unit_lists):
        for unit in group:
            if unit is not None:
                unit()


def _out_ffn_kernel(h_ref, attn_ref, yf_ref, yb_ref, xs_ref, z_ref, dskip_ref, gssd_ref,
                    wo_ref, gmix_ref, gpre_ref, wg_ref, wu_ref, wd_ref, gpost_ref,
                    gfinal_ref, o_ref, *scratch):
    heads = attn_ref.shape[1]
    tm, d = h_ref.shape
    half = tm // 2
    n_ff = wg_ref.shape[1]
    n_attn = heads * attn_ref.shape[3]
    per_half = len(scratch) // 2
    chunks = [slice(j * OUT_ROWS, (j + 1) * OUT_ROWS) for j in range(half // OUT_ROWS)]
    col_blocks = lambda n: [slice(j * MXU_COLS, (j + 1) * MXU_COLS) for j in range(n // MXU_COLS)]

    def stages(r):
        y_scr, m_scr, h2_scr, xn_scr, a_scr, f_scr = scratch[r * per_half:(r + 1) * per_half]
        base = r * half
        glob = lambda rows: slice(base + rows.start, base + rows.stop)

        def gate(rows):
            g = glob(rows)
            y = (yf_ref[g, :].astype(F32) + yb_ref[g, :].astype(F32)
                 + xs_ref[g, :].astype(F32) * dskip_ref[...])
            y = _rms(y * _silu(z_ref[g, :].astype(F32)), gssd_ref[...])
            y_scr[rows, :] = y.astype(BF16)

        def proj(cols):
            attn = jnp.concatenate([attn_ref[0, h, base:base + half, :] for h in range(heads)],
                                   axis=1)
            m_scr[:, cols] = (_dot(attn, wo_ref[:n_attn, cols])
                              + _dot(y_scr[...], wo_ref[n_attn:, cols]))

        def resid(rows):
            h2 = h_ref[glob(rows), :] + _rms(m_scr[rows, :], gmix_ref[...])
            h2_scr[rows, :] = h2
            xn_scr[rows, :] = _rms(h2, gpre_ref[...]).astype(BF16)

        def gate_up(cols):
            xn = xn_scr[...]
            a_scr[:, cols] = (_silu(_dot(xn, wg_ref[:, cols]))
                              * _dot(xn, wu_ref[:, cols])).astype(BF16)

        def down(cols):
            f_scr[:, cols] = _dot(a_scr[...], wd_ref[:, cols])

        def final(rows):
            h3 = h2_scr[rows, :] + 0.5 * _rms(f_scr[rows, :], gpost_ref[...])
            o_ref[glob(rows), :] = _rms(h3, gfinal_ref[...])

        bind = lambda fn, items: [functools.partial(fn, it) for it in items]
        return (bind(gate, chunks), bind(proj, col_blocks(d)), bind(resid, chunks),
                bind(gate_up, col_blocks(n_ff)), bind(down, col_blocks(d)), bind(final, chunks))

    a, b = stages(0), stages(1)
    _interleave(a[0])
    _interleave(a[1], b[0])
    _interleave(b[1], a[2])
    _interleave(a[3], b[2])
    _interleave(b[3], a[4])
    _interleave(b[4], a[5])
    _interleave(b[5])


def _out_ffn(h2d, attn, yf, yb, xconv, z, dskip, gssd, wo, gmix, gpre, wg, wu, wd, gpost,
             gfinal, tm):
    t, d = h2d.shape
    _, heads, seq, v_dim = attn.shape
    nb = seq // tm
    attn_spec = pl.BlockSpec((1, heads, tm, v_dim), lambda i: (i // nb, 0, i % nb, 0))
    f = wg.shape[1]
    n_x = yf.shape[-1]
    row = lambda n: pl.BlockSpec((tm, n), lambda i: (i, 0))
    return pl.pallas_call(
        _out_ffn_kernel,
        grid=(t // tm,),
        in_specs=[row(d), attn_spec, row(n_x), row(n_x), row(n_x), row(n_x),
                  _resident((1, n_x)), _resident((1, n_x)), _resident(wo.shape),
                  _resident((1, d)), _resident((1, d)), _resident((d, f)),
                  _resident((d, f)), _resident((f, d)), _resident((1, d)),
                  _resident((1, d))],
        out_specs=row(d),
        out_shape=jax.ShapeDtypeStruct((t, d), F32),
        scratch_shapes=2 * [pltpu.VMEM((tm // 2, n_x), BF16), pltpu.VMEM((tm // 2, d), F32),
                            pltpu.VMEM((tm // 2, d), F32), pltpu.VMEM((tm // 2, d), BF16),
                            pltpu.VMEM((tm // 2, f), BF16), pltpu.VMEM((tm // 2, d), F32)],
        compiler_params=_params("parallel"),
        name="out_ffn",
    )(h2d, attn, yf, yb, xconv, z, dskip, gssd, wo, gmix, gpre, wg, wu, wd, gpost, gfinal)


def _rotary_tables(seq):
    half = ROT_DIM // 2
    pos = jnp.arange(seq, dtype=F32)
    inv_freq = jnp.power(F32(ROPE_THETA), -jnp.arange(0, ROT_DIM, 2, dtype=F32) / ROT_DIM)
    ang = pos[:, None] * inv_freq[None, :]
    cos, sin = jnp.cos(ang), jnp.sin(ang)
    ones = jnp.ones((seq, ATTN_QK_DIM - ROT_DIM), F32)
    zeros_h = jnp.zeros((seq, half), F32)
    zeros_r = jnp.zeros((seq, ATTN_QK_DIM - ROT_DIM), F32)
    reps = LANES // ATTN_QK_DIM
    cos_t = jnp.tile(jnp.concatenate([cos, cos, ones], axis=1), (1, reps))
    sa_t = jnp.tile(jnp.concatenate([-sin, zeros_h, zeros_r], axis=1), (1, reps))
    sb_t = jnp.tile(jnp.concatenate([zeros_h, sin, zeros_r], axis=1), (1, reps))
    return cos_t, sa_t, sb_t


def _pad_lanes(v, width=LANES):
    return jnp.pad(v, ((0, 0), (0, width - v.shape[-1])))


def _layer(h2d, layer_idx, p, bsz, seq):
    t, d = h2d.shape
    tm = min(ROW_TILE, seq)
    row = lambda v: v.reshape(1, -1).astype(F32)
    bf = lambda w: w.astype(BF16)

    h1 = _ffn(h2d, row(p["ffn1_pre_g"]), bf(p["ffn1_w_gate"]), bf(p["ffn1_w_up"]),
              bf(p["ffn1_w_down"]), row(p["ffn1_post_g"]), tm)

    qk_cols = ATTN_HEADS * 2 * ATTN_QK_DIM
    v_cols = ATTN_HEADS * ATTN_V_DIM
    n_x = SSD_HEADS * SSD_HEAD_DIM
    xbc_cols = n_x + 2 * SSD_GROUPS * SSD_STATE
    w_in = p["w_in"]
    v0 = 2 * qk_cols
    wvt = bf(w_in[:, v0:v0 + v_cols].T)
    n_dt = w_in.shape[1] - (v0 + v_cols + n_x + xbc_cols)
    w_pad = bf(jnp.pad(jnp.concatenate([w_in[:, :v0], w_in[:, v0 + v_cols:]], axis=1),
                       ((0, 0), (0, LANES - n_dt))))
    dt_bias = _pad_lanes(jnp.concatenate([p["dt_bias_fwd"], p["dt_bias_bwd"]]).reshape(1, -1)
                         .astype(F32))
    alog = _pad_lanes(jnp.concatenate([p["a_log_fwd"], p["a_log_bwd"]]).reshape(1, -1)
                      .astype(F32))
    cos_t, sa_t, sb_t = _rotary_tables(seq)
    q, k, vt, z, xconv, dt = _in_proj(h1, row(p["mix_pre_g"]), w_pad, wvt, cos_t, sa_t, sb_t,
                                      dt_bias, p["conv_w"].astype(F32), row(p["conv_b"]), tm, seq,
                                      qk_cols, n_x, xbc_cols)
    shape3 = lambda a: a.reshape(bsz, seq, a.shape[-1])
    xconv = shape3(xconv)

    lam_init = 0.8 - 0.6 * math.exp(-0.3 * layer_idx)
    lam_params = jnp.stack([p["lambda_q1"], p["lambda_k1"], p["lambda_q2"],
                            p["lambda_k2"]]).astype(F32)
    attn = _attention(q, k, vt, lam_params, row(p["attn_subln_g"]), lam_init)

    yf, yb = _ssd(xconv, shape3(dt), alog)

    dskip = jnp.repeat(p["d_skip"].astype(F32), SSD_HEAD_DIM).reshape(1, -1)
    flat = lambda a: a.reshape(t, a.shape[-1])
    return _out_ffn(h1, attn, flat(yf), flat(yb), flat(xconv), z, dskip,
                    row(p["ssd_norm_g"]), bf(p["w_out"]), row(p["mix_post_g"]),
                    row(p["ffn2_pre_g"]), bf(p["ffn2_w_gate"]), bf(p["ffn2_w_up"]),
                    bf(p["ffn2_w_down"]), row(p["ffn2_post_g"]), row(p["final_g"]), tm)


def kernel(x, ffn1_pre_g, ffn1_w_gate, ffn1_w_up, ffn1_w_down, ffn1_post_g, mix_pre_g, w_in, lambda_q1, lambda_k1, lambda_q2, lambda_k2, attn_subln_g, conv_w, conv_b, a_log_fwd, a_log_bwd, dt_bias_fwd, dt_bias_bwd, d_skip, ssd_norm_g, w_out, mix_post_g, ffn2_pre_g, ffn2_w_gate, ffn2_w_up, ffn2_w_down, ffn2_post_g, final_g):
    names = ("ffn1_pre_g", "ffn1_w_gate", "ffn1_w_up", "ffn1_w_down", "ffn1_post_g",
             "mix_pre_g", "w_in", "lambda_q1", "lambda_k1", "lambda_q2", "lambda_k2",
             "attn_subln_g", "conv_w", "conv_b", "a_log_fwd", "a_log_bwd", "dt_bias_fwd",
             "dt_bias_bwd", "d_skip", "ssd_norm_g", "w_out", "mix_post_g", "ffn2_pre_g",
             "ffn2_w_gate", "ffn2_w_up", "ffn2_w_down", "ffn2_post_g", "final_g")
    stacked = dict(zip(names, (ffn1_pre_g, ffn1_w_gate, ffn1_w_up, ffn1_w_down, ffn1_post_g,
                               mix_pre_g, w_in, lambda_q1, lambda_k1, lambda_q2, lambda_k2,
                               attn_subln_g, conv_w, conv_b, a_log_fwd, a_log_bwd,
                               dt_bias_fwd, dt_bias_bwd, d_skip, ssd_norm_g, w_out,
                               mix_post_g, ffn2_pre_g, ffn2_w_gate, ffn2_w_up, ffn2_w_down,
                               ffn2_post_g, final_g)))
    bsz, seq, d = x.shape
    h = x.reshape(bsz * seq, d)
    for i in range(ffn1_pre_g.shape[0]):
        h = _layer(h, i, {n: a[i] for n, a in stacked.items()}, bsz, seq)
    return h.reshape(bsz, seq, d)
```
